```python
import math
import jax, jax.numpy as jnp
from jax import lax
import numpy as np

D_MODEL = 2048
BATCH = 4
SEQ = 2048
DEPTH = 4
DEC_BATCH = 128
DEC_SEQ = 4
PAST_LEN = 16384
PAGE_SIZE = 128

BRANCH_W = D_MODEL
N_BRANCH = 3
GLA_HEADS = 4
GLA_DV = BRANCH_W // GLA_HEADS
GLA_DK = GLA_DV // 2
GLA_QK = GLA_HEADS * GLA_DK
GLA_LORA = 16
GLA_GATE_TAU = 16.0
GLA_COLS = 2 * GLA_QK + 2 * BRANCH_W + GLA_LORA
GDN_HEAD = 128
GDN_HEADS = BRANCH_W // GDN_HEAD
GDN_QK = GDN_HEADS * GDN_HEAD
GDN_CONV = 4
GDN_CONV_COLS = 2 * GDN_QK + BRANCH_W
GDN_COLS = GDN_CONV_COLS + BRANCH_W + 2 * GDN_HEADS
RWKV_HEAD = 64
RWKV_HEADS = BRANCH_W // RWKV_HEAD
RWKV_W_LORA = 96
RWKV_A_LORA = 96
RWKV_G_LORA = 256
RWKV_COLS = 3 * BRANCH_W + RWKV_W_LORA + RWKV_A_LORA + RWKV_G_LORA
RWKV_GN_EPS = 64e-5
GATE_COLS = N_BRANCH * D_MODEL
IN_COLS = GLA_COLS + GDN_COLS + RWKV_COLS + GATE_COLS
CHUNK = 64
N_EXPERTS = 64
TOP_K = 8
N_GROUPS = 8
TOPK_GROUPS = 4
D_EXPERT = 512
ROUTED_SCALE = 2.5
MOE_BLOCK = 64
LN_EPS = 1e-5
RMS_EPS = 1e-6
L2_EPS = 1e-6
DEEPNORM_ALPHA = (2 * DEPTH) ** 0.25
DEEPNORM_BETA = (8 * DEPTH) ** -0.25

kernel_name = "hybrid_gla_gdn_rwkv7_moe_decoder_step"

F32 = jnp.float32


def _split(a, sizes):
    idx = [int(i) for i in np.cumsum(sizes)[:-1]]
    return jnp.split(a, idx, axis=-1)


def _heads(t, h):
    return t.reshape(t.shape[:-1] + (h, t.shape[-1] // h))


def _layernorm(x, g, b):
    xf = x.astype(F32)
    mu = xf.mean(-1, keepdims=True)
    var = jnp.square(xf - mu).mean(-1, keepdims=True)
    return ((xf - mu) * lax.rsqrt(var + LN_EPS) * g + b).astype(x.dtype)


def _rmsnorm(x, w):
    xf = x.astype(F32)
    return xf * lax.rsqrt(jnp.square(xf).mean(-1, keepdims=True) + RMS_EPS) * w


def _groupnorm_heads(y, w, b):
    mu = y.mean(-1, keepdims=True)
    var = jnp.square(y - mu).mean(-1, keepdims=True)
    h, d = y.shape[-2], y.shape[-1]
    return (y - mu) * lax.rsqrt(var + RWKV_GN_EPS) * w.reshape(h, d) + b.reshape(h, d)


def _l2norm(x):
    return x * lax.rsqrt(jnp.sum(jnp.square(x), -1, keepdims=True) + L2_EPS)


def _chunk_len(t):
    return math.gcd(t, CHUNK)


def _to_chunks(a, n, L):
    B, T, H, d = a.shape
    return a.reshape(B, n, L, H, d).transpose(1, 0, 3, 2, 4)


def _from_chunks(a):
    n, B, H, L, d = a.shape
    return a.transpose(1, 0, 3, 2, 4).reshape(B, n * L, H, d)


def _gla_chunked(q, k, v, log_a, S0):
    B, T, H, dk = q.shape
    L = _chunk_len(T)
    n = T // L
    q, k, v, log_a = (_to_chunks(t, n, L) for t in (q, k, v, log_a))
    G = jnp.cumsum(log_a, axis=-2)
    G_last = G[..., -1:, :]
    qg = q * (dk ** -0.5) * jnp.exp(G)
    kg = k * jnp.exp(-G)
    k_end = k * jnp.exp(G_last - G)
    causal = jnp.tril(jnp.ones((L, L), bool))
    A = jnp.where(causal, jnp.einsum('nbhld,nbhmd->nbhlm', qg, kg), 0.0)
    o_intra = jnp.einsum('nbhlm,nbhmv->nbhlv', A, v)

    def step(S, xs):
        qg_c, k_end_c, v_c, dec_c = xs
        o = jnp.einsum('bhld,bhdv->bhlv', qg_c, S)
        S = dec_c[..., 0, :, None] * S + jnp.einsum('bhld,bhlv->bhdv', k_end_c, v_c)
        return S, o

    S, o_inter = lax.scan(step, S0, (qg, k_end, v, jnp.exp(G_last)))
    return _from_chunks(o_intra + o_inter), S


def _gdn_chunked(q, k, v, g, beta, S0):
    B, T, H, dk = q.shape
    dv = v.shape[-1]
    L = _chunk_len(T)
    n = T // L
    q, k, v = (_to_chunks(t, n, L) for t in (q, k, v))
    g, beta = (t.reshape(B, n, L, H).transpose(1, 0, 3, 2) for t in (g, beta))
    q = q * (dk ** -0.5)
    Gc = jnp.cumsum(g, axis=-1)
    causal = jnp.tril(jnp.ones((L, L), bool))
    strict = jnp.tril(jnp.ones((L, L), bool), k=-1)
    diff = Gc[..., :, None] - Gc[..., None, :]
    decay_mask = jnp.where(causal, jnp.exp(jnp.where(causal, diff, 0.0)), 0.0)
    k_beta = k * beta[..., None]
    v_beta = v * beta[..., None]
    M = jnp.where(strict, jnp.einsum('nbhld,nbhmd->nbhlm', k_beta, k) * decay_mask, 0.0)
    rhs = jnp.concatenate([v_beta, k_beta * jnp.exp(Gc)[..., None]], axis=-1)
    sol = lax.linalg.triangular_solve(M + jnp.eye(L, dtype=M.dtype), rhs, left_side=True,
                                      lower=True, unit_diagonal=True)
    u, w = sol[..., :dv], sol[..., dv:]
    attn = jnp.einsum('nbhld,nbhmd->nbhlm', q, k) * decay_mask
    q_dec = q * jnp.exp(Gc)[..., None]
    k_end = k * jnp.exp(Gc[..., -1:] - Gc)[..., None]
    dec_last = jnp.exp(Gc[..., -1])

    def step(S, xs):
        q_c, w_c, u_c, attn_c, k_end_c, d_c = xs
        v_new = u_c - jnp.einsum('bhld,bhdv->bhlv', w_c, S)
        o = jnp.einsum('bhld,bhdv->bhlv', q_c, S) + jnp.einsum('bhlm,bhmv->bhlv', attn_c, v_new)
        S = d_c[..., None, None] * S + jnp.einsum('bhld,bhlv->bhdv', k_end_c, v_new)
        return S, o

    S, o = lax.scan(step, S0, (q_dec, w, u, attn, k_end, dec_last))
    return _from_chunks(o), S


def _rwkv7_scan(r, w, k, v, a, b, S0):
    def step(S, xs):
        r_t, w_t, k_t, v_t, a_t, b_t = xs
        sa = jnp.einsum('bhij,bhj->bhi', S, a_t)
        S = (S * w_t[:, :, None, :] + sa[..., None] * b_t[:, :, None, :]
             + v_t[..., None] * k_t[:, :, None, :])
        return S, jnp.einsum('bhij,bhj->bhi', S, r_t)

    xs = tuple(t.transpose(1, 0, 2, 3) for t in (r, w, k, v, a, b))
    S, y = lax.scan(step, S0, xs)
    return y.transpose(1, 0, 2, 3), S


def _causal_conv(u, buf, w):
    full = jnp.concatenate([buf.astype(u.dtype), u], axis=1)
    y = lax.conv_general_dilated(full, w[:, None, :].astype(u.dtype), window_strides=(1,),
                                 padding='VALID', dimension_numbers=('NWC', 'WIO', 'NWC'),
                                 feature_group_count=u.shape[-1])
    return jax.nn.silu(y), full[:, -(GDN_CONV - 1):]


def _route(h2, w_router, router_bias):
    N = h2.shape[0]
    scores = jax.nn.sigmoid(h2.astype(F32) @ w_router.astype(F32))
    sel = scores + router_bias.astype(F32)
    grp = sel.reshape(N, N_GROUPS, N_EXPERTS // N_GROUPS)
    grp_score = lax.top_k(grp, 2)[0].sum(-1)
    _, gidx = lax.top_k(grp_score, TOPK_GROUPS)
    gmask = jnp.any(gidx[..., None] == jnp.arange(N_GROUPS), axis=-2)
    emask = jnp.repeat(gmask, N_EXPERTS // N_GROUPS, axis=-1)
    _, ids = lax.top_k(jnp.where(emask, sel, -jnp.inf), TOP_K)
    wts = jnp.take_along_axis(scores, ids, axis=-1)
    wts = wts / (wts.sum(-1, keepdims=True) + 1e-20) * ROUTED_SCALE
    return ids, wts


def _routed_experts(h2, ids, wts, w_gu, w_down):
    N, D = h2.shape
    NK = N * TOP_K
    flat_e = ids.reshape(NK)
    flat_w = wts.reshape(NK)
    flat_tok = jnp.arange(NK, dtype=jnp.int32) // TOP_K
    order = jnp.argsort(flat_e)
    se = flat_e[order]
    counts = jnp.bincount(flat_e, length=N_EXPERTS)
    padded = (counts + MOE_BLOCK - 1) // MOE_BLOCK * MOE_BLOCK
    pad_end = jnp.cumsum(padded)
    pad_start = pad_end - padded
    start = jnp.cumsum(counts) - counts
    dest = pad_start[se] + jnp.arange(NK, dtype=jnp.int32) - start[se]
    n_blocks = -(-NK // MOE_BLOCK) + N_EXPERTS
    P = n_blocks * MOE_BLOCK
    slot_tok = jnp.full((P,), N, jnp.int32).at[dest].set(flat_tok[order])
    slot_w = jnp.zeros((P,), F32).at[dest].set(flat_w[order])
    blk_start = jnp.arange(n_blocks, dtype=jnp.int32) * MOE_BLOCK
    blk_e = jnp.minimum(jnp.searchsorted(pad_end, blk_start, side='right'), N_EXPERTS - 1)
    h_pad = jnp.concatenate([h2, jnp.zeros((1, D), h2.dtype)], axis=0)

    def block(args):
        tok, e = args
        hb = h_pad[tok]
        gt, up = jnp.split(hb @ w_gu[e], 2, axis=-1)
        return (jax.nn.silu(gt) * up) @ w_down[e]

    yb = lax.map(block, (slot_tok.reshape(n_blocks, MOE_BLOCK), blk_e))
    out = jnp.zeros((N + 1, D), yb.dtype).at[slot_tok].add(
        yb.reshape(P, D) * slot_w[:, None].astype(yb.dtype))
    return out[:N]


def _swiglu(h2, w_gu, w_down):
    gt, up = jnp.split(h2 @ w_gu, 2, axis=-1)
    return (jax.nn.silu(gt) * up) @ w_down


def _layer(x, states, p):
    gla_S, gdn_S, conv_buf, rwkv_S, shift = states
    B, T, D = x.shape
    u = x @ p['w_in']
    u_gla, u_gdn, u_rwkv, u_gate = _split(u, [GLA_COLS, GDN_COLS, RWKV_COLS, GATE_COLS])

    gq, gk, gv, gog, glr = _split(u_gla, [GLA_QK, GLA_QK, BRANCH_W, BRANCH_W, GLA_LORA])
    log_a = jax.nn.log_sigmoid((glr @ p['gla_w_lora'] + p['gla_b_lora']).astype(F32)) / GLA_GATE_TAU
    o, gla_S = _gla_chunked(_heads(gq.astype(F32), GLA_HEADS), _heads(gk.astype(F32), GLA_HEADS),
                            _heads(gv.astype(F32), GLA_HEADS), _heads(log_a, GLA_HEADS),
                            gla_S.astype(F32))
    o_gla = (_rmsnorm(o, p['gla_norm_w'])
             * jax.nn.silu(_heads(gog.astype(F32), GLA_HEADS))).reshape(B, T, BRANCH_W)

    qkv, gz, gb, ga = _split(u_gdn, [GDN_CONV_COLS, BRANCH_W, GDN_HEADS, GDN_HEADS])
    qkv, conv_buf = _causal_conv(qkv, conv_buf, p['gdn_conv_w'])
    dq, dk, dv = _split(qkv.astype(F32), [GDN_QK, GDN_QK, BRANCH_W])
    beta = jax.nn.sigmoid(gb.astype(F32))
    g = -jnp.exp(p['gdn_A_log'].astype(F32)) * jax.nn.softplus(ga.astype(F32) + p['gdn_dt_bias'])
    o, gdn_S = _gdn_chunked(_l2norm(_heads(dq, GDN_HEADS)), _l2norm(_heads(dk, GDN_HEADS)),
                            _heads(dv, GDN_HEADS), g, beta, gdn_S.astype(F32))
    o_gdn = (_rmsnorm(o, p['gdn_norm_w'])
             * jax.nn.silu(_heads(gz.astype(F32), GDN_HEADS))).reshape(B, T, BRANCH_W)

    prev = jnp.concatenate([shift[:, None, :].astype(u_rwkv.dtype), u_rwkv[:, :-1]], axis=1)
    new_shift = u_rwkv[:, -1]
    xm = (u_rwkv + (prev - u_rwkv) * p['rwkv_mu']).astype(F32)
    r, k, v, wl, al, gl = _split(xm, [BRANCH_W] * 3 + [RWKV_W_LORA, RWKV_A_LORA, RWKV_G_LORA])
    w_log = -jax.nn.softplus(-(p['rwkv_w0'] + jnp.tanh(wl) @ p['rwkv_w2'])) - 0.5
    decay = jnp.exp(-jnp.exp(w_log))
    a = jax.nn.sigmoid(p['rwkv_a0'] + al @ p['rwkv_a2'])
    gate = jax.nn.sigmoid(gl) @ p['rwkv_g2']
    kk = _l2norm(_heads(k * p['rwkv_k_k'], RWKV_HEADS))
    k = k * (1.0 + (a - 1.0) * p['rwkv_k_a'])
    rh, kh, vh, ah = (_heads(t, RWKV_HEADS) for t in (r, k, v, a))
    y, rwkv_S = _rwkv7_scan(rh, _heads(decay, RWKV_HEADS), kh, vh, -kk, kk * ah, rwkv_S.astype(F32))
    y = (_groupnorm_heads(y, p['rwkv_ln_w'], p['rwkv_ln_b'])
         + jnp.sum(rh * kh * p['rwkv_r_k'], -1, keepdims=True) * vh)
    o_rwkv = y.reshape(B, T, BRANCH_W) * gate

    gates = jax.nn.sigmoid(u_gate.astype(F32)).reshape(B, T, N_BRANCH, D)
    o_all = jnp.stack([o_gla, o_gdn, o_rwkv], axis=2).astype(x.dtype)
    branch = jnp.einsum('btnc,ncd->btnd', o_all, p['w_branch'])
    merged = (gates * branch).sum(2).astype(x.dtype)
    h = _layernorm(DEEPNORM_ALPHA * x + merged @ p['w_out'], p['ln1_g'], p['ln1_b'])

    h2 = h.reshape(B * T, D)
    ids, wts = _route(h2, p['w_router'], p['router_bias'])
    moe = (_routed_experts(h2, ids, wts, p['w_exp_gu'], p['w_exp_down'])
           + _swiglu(h2, p['w_sh_gu'], p['w_sh_down']))
    x_out = _layernorm(DEEPNORM_ALPHA * h + moe.reshape(B, T, D).astype(h.dtype), p['ln2_g'], p['ln2_b'])
    return x_out, (gla_S, gdn_S, conv_buf, rwkv_S, new_shift)


def setup_inputs(seed: int = 0) -> dict:
    key = jax.random.key(seed)
    ks = iter(jax.random.split(key, 48))
    L = DEPTH

    def nrm(shape, scale):
        return jax.random.normal(next(ks), shape, F32) * scale

    def unif(shape, lo, hi):
        return jax.random.uniform(next(ks), shape, F32, lo, hi)

    ramp = jnp.linspace(0.0, 1.0, BRANCH_W, dtype=F32) ** 0.7
    dt = jnp.exp(unif((L, GDN_HEADS), math.log(1e-3), math.log(1e-1)))
    return {
        "x_prompt": nrm((BATCH, SEQ, D_MODEL), 1.0),
        "x_sample": nrm((DEC_BATCH, DEC_SEQ, D_MODEL), 1.0),
        "state_gla": nrm((L, DEC_BATCH, GLA_HEADS, GLA_DK, GLA_DV), 0.05),
        "state_gdn": nrm((L, DEC_BATCH, GDN_HEADS, GDN_HEAD, GDN_HEAD), 0.05),
        "state_gdn_conv": nrm((L, DEC_BATCH, GDN_CONV - 1, GDN_CONV_COLS), 1.0),
        "state_rwkv": nrm((L, DEC_BATCH, RWKV_HEADS, RWKV_HEAD, RWKV_HEAD), 0.05),
        "state_rwkv_shift": nrm((L, DEC_BATCH, RWKV_COLS), 1.0),
        "w_in": nrm((L, D_MODEL, IN_COLS), D_MODEL ** -0.5),
        "gla_w_lora": nrm((L, GLA_LORA, GLA_QK), GLA_LORA ** -0.5),
        "gla_b_lora": nrm((L, GLA_QK), 0.1),
        "gla_norm_w": 1.0 + nrm((L, GLA_DV), 0.02),
        "gdn_conv_w": nrm((L, GDN_CONV, GDN_CONV_COLS), 0.5),
        "gdn_A_log": jnp.log(unif((L, GDN_HEADS), 1.0, 16.0)),
        "gdn_dt_bias": jnp.log(jnp.expm1(dt)),
        "gdn_norm_w": 1.0 + nrm((L, GDN_HEAD), 0.02),
        "rwkv_mu": unif((L, RWKV_COLS), 0.0, 1.0),
        "rwkv_w0": -6.0 + 5.0 * ramp + nrm((L, BRANCH_W), 0.1),
        "rwkv_w2": nrm((L, RWKV_W_LORA, BRANCH_W), 0.1 * RWKV_W_LORA ** -0.5),
        "rwkv_a0": nrm((L, BRANCH_W), 0.1),
        "rwkv_a2": nrm((L, RWKV_A_LORA, BRANCH_W), 0.5 * RWKV_A_LORA ** -0.5),
        "rwkv_g2": nrm((L, RWKV_G_LORA, BRANCH_W), RWKV_G_LORA ** -0.5),
        "rwkv_k_k": 0.85 + nrm((L, BRANCH_W), 0.02),
        "rwkv_k_a": 1.0 + nrm((L, BRANCH_W), 0.02),
        "rwkv_r_k": nrm((L, RWKV_HEADS, RWKV_HEAD), 0.1),
        "rwkv_ln_w": 1.0 + nrm((L, BRANCH_W), 0.02),
        "rwkv_ln_b": nrm((L, BRANCH_W), 0.02),
        "w_branch": nrm((L, N_BRANCH, BRANCH_W, D_MODEL), BRANCH_W ** -0.5),
        "w_out": nrm((L, D_MODEL, D_MODEL), D_MODEL ** -0.5 * DEEPNORM_BETA),
        "ln1_g": 1.0 + nrm((L, D_MODEL), 0.02),
        "ln1_b": nrm((L, D_MODEL), 0.02),
        "w_router": nrm((L, D_MODEL, N_EXPERTS), D_MODEL ** -0.5),
        "router_bias": nrm((L, N_EXPERTS), 0.01),
        "w_exp_gu": nrm((L, N_EXPERTS, D_MODEL, 2 * D_EXPERT), D_MODEL ** -0.5),
        "w_exp_down": nrm((L, N_EXPERTS, D_EXPERT, D_MODEL), D_EXPERT ** -0.5 * DEEPNORM_BETA),
        "w_sh_gu": nrm((L, D_MODEL, 2 * D_EXPERT), D_MODEL ** -0.5),
        "w_sh_down": nrm((L, D_EXPERT, D_MODEL), D_EXPERT ** -0.5 * DEEPNORM_BETA),
        "ln2_g": 1.0 + nrm((L, D_MODEL), 0.02),
        "ln2_b": nrm((L, D_MODEL), 0.02),
    }


def reference(x_prompt, x_sample, state_gla, state_gdn, state_gdn_conv, state_rwkv, state_rwkv_shift,
              w_in, gla_w_lora, gla_b_lora, gla_norm_w, gdn_conv_w, gdn_A_log, gdn_dt_bias, gdn_norm_w,
              rwkv_mu, rwkv_w0, rwkv_w2, rwkv_a0, rwkv_a2, rwkv_g2, rwkv_k_k, rwkv_k_a, rwkv_r_k,
              rwkv_ln_w, rwkv_ln_b, w_branch, w_out, ln1_g, ln1_b, w_router, router_bias,
              w_exp_gu, w_exp_down, w_sh_gu, w_sh_down, ln2_g, ln2_b):
    names = ('w_in', 'gla_w_lora', 'gla_b_lora', 'gla_norm_w', 'gdn_conv_w', 'gdn_A_log',
             'gdn_dt_bias', 'gdn_norm_w', 'rwkv_mu', 'rwkv_w0', 'rwkv_w2', 'rwkv_a0', 'rwkv_a2',
             'rwkv_g2', 'rwkv_k_k', 'rwkv_k_a', 'rwkv_r_k', 'rwkv_ln_w', 'rwkv_ln_b', 'w_branch',
             'w_out', 'ln1_g', 'ln1_b', 'w_router', 'router_bias', 'w_exp_gu', 'w_exp_down',
             'w_sh_gu', 'w_sh_down', 'ln2_g', 'ln2_b')
    arrays = (w_in, gla_w_lora, gla_b_lora, gla_norm_w, gdn_conv_w, gdn_A_log, gdn_dt_bias,
              gdn_norm_w, rwkv_mu, rwkv_w0, rwkv_w2, rwkv_a0, rwkv_a2, rwkv_g2, rwkv_k_k, rwkv_k_a,
              rwkv_r_k, rwkv_ln_w, rwkv_ln_b, w_branch, w_out, ln1_g, ln1_b, w_router, router_bias,
              w_exp_gu, w_exp_down, w_sh_gu, w_sh_down, ln2_g, ln2_b)
    Bp = x_prompt.shape[0]
    zero_states = (jnp.zeros((Bp, GLA_HEADS, GLA_DK, GLA_DV), F32),
                   jnp.zeros((Bp, GDN_HEADS, GDN_HEAD, GDN_HEAD), F32),
                   jnp.zeros((Bp, GDN_CONV - 1, GDN_CONV_COLS), x_prompt.dtype),
                   jnp.zeros((Bp, RWKV_HEADS, RWKV_HEAD, RWKV_HEAD), F32),
                   jnp.zeros((Bp, RWKV_COLS), x_prompt.dtype))
    xp, xs = x_prompt, x_sample
    new_p, new_s = [], []
    for l in range(DEPTH):
        p = {n: a[l] for n, a in zip(names, arrays)}
        xp, st_p = _layer(xp, zero_states, p)
        xs, st_s = _layer(xs, (state_gla[l], state_gdn[l], state_gdn_conv[l], state_rwkv[l],
                               state_rwkv_shift[l]), p)
        new_p.append(st_p)
        new_s.append(st_s)
    gla_p, gdn_p, conv_p, rwkv_p, shift_p = (jnp.stack([s[i] for s in new_p]) for i in range(5))
    gla_s, gdn_s, conv_s, rwkv_s, shift_s = (jnp.stack([s[i] for s in new_s]) for i in range(5))
    return (xp, xs, gla_p, gdn_p, conv_p, rwkv_p, shift_p, gla_s, gdn_s, conv_s, rwkv_s, shift_s)
```

```python
import functools
import math

import jax
import jax.numpy as jnp
from jax import lax
from jax.experimental import pallas as pl
from jax.experimental.pallas import tpu as pltpu

F32 = jnp.float32
BF16 = jnp.bfloat16
HIGHEST = lax.Precision.HIGHEST

GLA_HEADS = 4
GLA_GATE_TAU = 16.0
GDN_HEAD = 128
GDN_CONV = 4
RWKV_HEAD = 64
RWKV_GN_EPS = 64e-5
CHUNK = 64
TOP_K = 8
N_GROUPS = 8
TOPK_GROUPS = 4
ROUTED_SCALE = 2.5
LN_EPS = 1e-5
RMS_EPS = 1e-6
L2_EPS = 1e-6

LANES = 128
SUBLANES = 8
VMEM_LIMIT = 56 * 1024 * 1024

MISC_WL = 128
MISC_AL = 256
MISC_GL = 384


def _pick(n, cands):
    for c in cands:
        if n % c == 0:
            return c
    raise ValueError(f"no tile for {n}")


def _params(sem):
    return pltpu.CompilerParams(dimension_semantics=sem, vmem_limit_bytes=VMEM_LIMIT)


def _bdot(a, b):
    return jnp.dot(a.astype(BF16), b.astype(BF16), preferred_element_type=F32)


def _bdot_nt(a, b):
    return lax.dot_general(a.astype(BF16), b.astype(BF16), (((1,), (1,)), ((), ())), preferred_element_type=F32)


def _bdot_tn(a, b):
    return lax.dot_general(a.astype(BF16), b.astype(BF16), (((0,), (0,)), ((), ())), preferred_element_type=F32)


def _hdot(a, b):
    return jnp.dot(a, b, precision=HIGHEST, preferred_element_type=F32)


def _softplus(x):
    return jnp.maximum(x, 0.0) + jnp.log1p(jnp.exp(-jnp.abs(x)))


def _sigmoid(x):
    return 1.0 / (1.0 + jnp.exp(-x))


def _tri_masks(L):
    ri = lax.broadcasted_iota(jnp.int32, (L, L), 0)
    ci = lax.broadcasted_iota(jnp.int32, (L, L), 1)
    return ri >= ci, ri > ci, ri == ci


def _neumann_inverse(x, eye, nsq):
    inv = eye + x
    p = x
    for _ in range(nsq):
        p = _hdot(p, p)
        inv = inv + _hdot(p, inv)
    return inv


def _mm_kernel(x_ref, w_ref, o_ref):
    o_ref[...] = jnp.dot(x_ref[...], w_ref[...], preferred_element_type=F32)


def _matmul(x, w, tm, tn):
    n, k = x.shape
    c = w.shape[1]
    return pl.pallas_call(
        _mm_kernel,
        grid=(n // tm, c // tn),
        in_specs=[pl.BlockSpec((tm, k), lambda i, j: (i, 0)),
                  pl.BlockSpec((k, tn), lambda i, j: (0, j))],
        out_specs=pl.BlockSpec((tm, tn), lambda i, j: (i, j)),
        out_shape=jax.ShapeDtypeStruct((n, c), F32),
        compiler_params=_params(("parallel", "parallel")),
        name="in_proj",
    )(x, w)


def _gla_kernel(q_ref, k_ref, v_ref, og_ref, misc_ref, wl_ref, bl_ref, nw_ref, s0_ref, o_ref, s_ref,
                *, L, t_real):
    c = pl.program_id(2)

    @pl.when(c == 0)
    def _():
        s_ref[0, 0] = s0_ref[0, 0, 0]

    dk = q_ref.shape[1]
    row = lax.broadcasted_iota(jnp.int32, (L, 1), 0)
    valid = (c * L + row) < t_real
    causal, _, _ = _tri_masks(L)

    z = _bdot(misc_ref[...], wl_ref[...]) + bl_ref[...]
    log_a = jnp.where(valid, -_softplus(-z) / GLA_GATE_TAU, 0.0)
    q = q_ref[...]
    k = jnp.where(valid, k_ref[...], 0.0)
    v = jnp.where(valid, v_ref[...], 0.0)

    g = _hdot(causal.astype(F32), log_a)
    g_last = g[L - 1:L, :]
    qg = q * (dk ** -0.5) * jnp.exp(g)
    kg = k * jnp.exp(-g)
    k_end = k * jnp.exp(g_last - g)
    a = jnp.where(causal, _bdot_nt(qg, kg), 0.0)
    s = s_ref[0, 0]
    o = _bdot(a, v) + _bdot(qg, s)

    ri = lax.broadcasted_iota(jnp.int32, (dk, dk), 0)
    ci = lax.broadcasted_iota(jnp.int32, (dk, dk), 1)
    dec_col = jnp.sum(jnp.where(ri == ci, jnp.broadcast_to(jnp.exp(g_last), (dk, dk)), 0.0), axis=1, keepdims=True)
    s_ref[0, 0] = dec_col * s + _bdot_tn(k_end, v)

    rms = o * lax.rsqrt(jnp.mean(o * o, axis=-1, keepdims=True) + RMS_EPS) * nw_ref[...]
    og = og_ref[...]
    o_ref[...] = rms * (og * _sigmoid(og))


def _gla_call(u_main, u_misc, wl_ext, b_lora, norm_w, s0, lyr, *, grp, D):
    row0, B, Tp, L, t_real = grp
    H = GLA_HEADS
    dv = D // H
    dk = dv // 2
    nc = Tp // L
    rb = lambda b, c: (row0 + b * Tp) // L + c
    return pl.pallas_call(
        functools.partial(_gla_kernel, L=L, t_real=t_real),
        grid=(B, H, nc),
        in_specs=[
            pl.BlockSpec((L, dk), lambda b, h, c: (rb(b, c), h)),
            pl.BlockSpec((L, dk), lambda b, h, c: (rb(b, c), (D // 2) // dk + h)),
            pl.BlockSpec((L, dv), lambda b, h, c: (rb(b, c), D // dv + h)),
            pl.BlockSpec((L, dv), lambda b, h, c: (rb(b, c), 2 * D // dv + h)),
            pl.BlockSpec((L, LANES), lambda b, h, c: (rb(b, c), 0)),
            pl.BlockSpec((LANES, dk), lambda b, h, c: (0, h)),
            pl.BlockSpec((1, dk), lambda b, h, c: (0, h)),
            pl.BlockSpec((1, dv), lambda b, h, c: (0, 0)),
            pl.BlockSpec((1, 1, 1, dk, dv), lambda b, h, c: (lyr, b, h, 0, 0)),
        ],
        out_specs=[
            pl.BlockSpec((L, dv), lambda b, h, c: (b * nc + c, h)),
            pl.BlockSpec((1, 1, dk, dv), lambda b, h, c: (b, h, 0, 0)),
        ],
        out_shape=[jax.ShapeDtypeStruct((B * Tp, D), F32),
                   jax.ShapeDtypeStruct((B, H, dk, dv), F32)],
        compiler_params=_params(("parallel", "parallel", "arbitrary")),
        name="gla_mixer",
    )(u_main, u_main, u_main, u_main, u_misc, wl_ext, b_lora, norm_w, s0)


def _gdn_kernel(q_ref, k_ref, v_ref, pq_ref, pk_ref, pv_ref, iq_ref, ik_ref, iv_ref,
                cq_ref, ck_ref, cv_ref, gz_ref, misc_ref, prm_ref, nw_ref, s0_ref, o_ref, s_ref,
                *, L, t_real, hb, lane_b, lane_a, nsq):
    c = pl.program_id(2)
    hg0 = pl.program_id(1) * hb

    @pl.when(c == 0)
    def _():
        s_ref[0] = s0_ref[0, 0]

    row = lax.broadcasted_iota(jnp.int32, (L, 1), 0)
    valid = (c * L + row) < t_real
    causal, strict, eye = _tri_masks(L)
    eye_f = eye.astype(F32)

    def conv(cur_ref, prev_ref, init_ref, w_ref):
        cur = cur_ref[...]
        prev = jnp.where(c == 0, init_ref[0], prev_ref[...])
        ext = jnp.concatenate([prev, cur], axis=0)
        w = w_ref[...]
        y = cur * w[GDN_CONV - 1:GDN_CONV]
        for j in range(1, GDN_CONV):
            y = y + pltpu.roll(ext, j, 0)[SUBLANES:] * w[GDN_CONV - 1 - j:GDN_CONV - j]
        return y * _sigmoid(y)

    qc = conv(q_ref, pq_ref, iq_ref, cq_ref)
    kc = conv(k_ref, pk_ref, ik_ref, ck_ref)
    vc = conv(v_ref, pv_ref, iv_ref, cv_ref)

    misc = misc_ref[...]
    prm = prm_ref[...]
    beta_all = jnp.where(valid, _sigmoid(misc), 0.0)
    g_all = jnp.where(valid, -jnp.exp(prm[0:1]) * _softplus(misc + prm[1:2]), 0.0)
    gc_all = _hdot(causal.astype(F32), g_all)
    gc_all_t = gc_all.T
    lane = lax.broadcasted_iota(jnp.int32, (L, LANES), 1)
    sub = lax.broadcasted_iota(jnp.int32, (LANES, L), 0)
    gz = gz_ref[...]
    nw = nw_ref[...]
    hd = GDN_HEAD

    for j in range(hb):
        hg = hg0 + j
        sl = slice(j * hd, (j + 1) * hd)
        beta = jnp.sum(jnp.where(lane == lane_b + hg, beta_all, 0.0), axis=1, keepdims=True)
        gc = jnp.sum(jnp.where(lane == lane_a + hg, gc_all, 0.0), axis=1, keepdims=True)
        gr = jnp.sum(jnp.where(sub == lane_a + hg, gc_all_t, 0.0), axis=0, keepdims=True)
        gc_last = gc[L - 1:L]
        q = qc[:, sl]
        k = kc[:, sl]
        v = vc[:, sl]
        q = q * lax.rsqrt(jnp.sum(q * q, axis=-1, keepdims=True) + L2_EPS) * (hd ** -0.5)
        k = k * lax.rsqrt(jnp.sum(k * k, axis=-1, keepdims=True) + L2_EPS)
        decay = jnp.where(causal, jnp.exp(jnp.where(causal, gc - gr, 0.0)), 0.0)
        k_beta = k * beta
        v_beta = v * beta
        m = jnp.where(strict, _bdot_nt(k_beta, k) * decay, 0.0)
        inv = _neumann_inverse(-m, eye_f, nsq)
        u = _hdot(inv, v_beta)
        w = _hdot(inv, k_beta * jnp.exp(gc))
        attn = _bdot_nt(q, k) * decay
        q_dec = q * jnp.exp(gc)
        k_end = k * jnp.exp(gc_last - gc)
        s = s_ref[0, j]
        v_new = u - _bdot(w, s)
        o = _bdot(q_dec, s) + _bdot(attn, v_new)
        s_ref[0, j] = jnp.exp(gc_last) * s + _bdot_tn(k_end, v_new)
        rms = o * lax.rsqrt(jnp.mean(o * o, axis=-1, keepdims=True) + RMS_EPS) * nw
        z = gz[:, sl]
        o_ref[:, sl] = rms * (z * _sigmoid(z))


def _gdn_call(u_main, u_misc, conv_init, conv_w, prm, norm_w, s0, lyr, *, grp, D, lane_b, lane_a, hb):
    row0, B, Tp, L, t_real = grp
    H = D // GDN_HEAD
    hb = min(hb, H)
    W = hb * GDN_HEAD
    nc = Tp // L
    rb = lambda b, c: (row0 + b * Tp) // L + c
    pb = lambda b, c: jnp.maximum((row0 + b * Tp + c * L) // SUBLANES - 1, 0)
    off = 3 * D // W
    seg = D // W
    cur = lambda s: pl.BlockSpec((L, W), lambda b, h, c: (rb(b, c), off + s * seg + h))
    prev = lambda s: pl.BlockSpec((SUBLANES, W), lambda b, h, c: (pb(b, c), off + s * seg + h))
    init = lambda s: pl.BlockSpec((1, SUBLANES, W), lambda b, h, c: (b, 0, s * seg + h))
    cw = lambda s: pl.BlockSpec((GDN_CONV, W), lambda b, h, c: (0, s * seg + h))
    nsq = int(math.log2(L)) - 1
    return pl.pallas_call(
        functools.partial(_gdn_kernel, L=L, t_real=t_real, hb=hb, lane_b=lane_b, lane_a=lane_a, nsq=nsq),
        grid=(B, H // hb, nc),
        in_specs=[cur(0), cur(1), cur(2), prev(0), prev(1), prev(2), init(0), init(1), init(2),
                  cw(0), cw(1), cw(2),
                  pl.BlockSpec((L, W), lambda b, h, c: (rb(b, c), 6 * D // W + h)),
                  pl.BlockSpec((L, LANES), lambda b, h, c: (rb(b, c), 0)),
                  pl.BlockSpec((SUBLANES, LANES), lambda b, h, c: (0, 0)),
                  pl.BlockSpec((1, GDN_HEAD), lambda b, h, c: (0, 0)),
                  pl.BlockSpec((1, 1, hb, GDN_HEAD, GDN_HEAD), lambda b, h, c: (lyr, b, h, 0, 0))],
        out_specs=[pl.BlockSpec((L, W), lambda b, h, c: (b * nc + c, h)),
                   pl.BlockSpec((1, hb, GDN_HEAD, GDN_HEAD), lambda b, h, c: (b, h, 0, 0))],
        out_shape=[jax.ShapeDtypeStruct((B * Tp, D), F32),
                   jax.ShapeDtypeStruct((B, H, GDN_HEAD, GDN_HEAD), F32)],
        compiler_params=_params(("parallel", "parallel", "arbitrary")),
        name="gdn_mixer",
    )(u_main, u_main, u_main, u_main, u_main, u_main, conv_init, conv_init, conv_init,
      conv_w, conv_w, conv_w, u_main, u_misc, prm, norm_w, s0)


def _rwkv_kernel(r_ref, k_ref, v_ref, pr_ref, pk_ref, pv_ref, ir_ref, ik_ref, iv_ref,
                 misc_ref, pmisc_ref, imisc_ref, mur_ref, muk_ref, muv_ref, mum_ref,
                 w0_ref, w2_ref, a0_ref, a2_ref, g2_ref, kk_ref, ka_ref, rk_ref, lnw_ref, lnb_ref, s0_ref,
                 o_ref, s_ref, *, L, t_real, hb, nsq):
    c = pl.program_id(2)

    @pl.when(c == 0)
    def _():
        s_ref[0] = s0_ref[0, 0]

    row = lax.broadcasted_iota(jnp.int32, (L, 1), 0)
    valid = (c * L + row) < t_real
    causal, strict, eye = _tri_masks(L)
    eye_f = eye.astype(F32)

    def mix(cur_ref, prev_ref, init_ref, mu_ref):
        cur = cur_ref[...]
        prev8 = jnp.where(c == 0, init_ref[0], prev_ref[...])
        ext = jnp.concatenate([prev8, cur], axis=0)
        prev = pltpu.roll(ext, 1, 0)[SUBLANES:]
        return cur + (prev - cur) * mu_ref[...]

    r = mix(r_ref, pr_ref, ir_ref, mur_ref)
    k = mix(k_ref, pk_ref, ik_ref, muk_ref)
    v = mix(v_ref, pv_ref, iv_ref, muv_ref)
    m = mix(misc_ref, pmisc_ref, imisc_ref, mum_ref)
    wl = m[:, MISC_WL:MISC_AL]
    al = m[:, MISC_AL:MISC_GL]
    gl = m[:, MISC_GL:]

    w_log = -_softplus(-(w0_ref[...] + _bdot(jnp.tanh(wl), w2_ref[...]))) - 0.5
    lw = jnp.where(valid, -jnp.exp(w_log), 0.0)
    a_sig = _sigmoid(a0_ref[...] + _bdot(al, a2_ref[...]))
    gate = _bdot(_sigmoid(gl), g2_ref[...])
    kk = k * kk_ref[...]
    k2 = jnp.where(valid, k * (1.0 + (a_sig - 1.0) * ka_ref[...]), 0.0)
    rk = rk_ref[...]
    lnw = lnw_ref[...]
    lnb = lnb_ref[...]
    cw = _hdot(causal.astype(F32), lw)
    hd = RWKV_HEAD

    for j in range(hb):
        sl = slice(j * hd, (j + 1) * hd)
        cw_h = cw[:, sl]
        cw_last = cw_h[L - 1:L]
        kk_h = kk[:, sl]
        kk_h = jnp.where(valid, kk_h * lax.rsqrt(jnp.sum(kk_h * kk_h, axis=-1, keepdims=True) + L2_EPS), 0.0)
        r_h = r[:, sl]
        k_h = k2[:, sl]
        v_h = v[:, sl]
        a_h = -kk_h
        b_h = kk_h * a_sig[:, sl]
        e_neg = jnp.exp(-cw_h)
        e_end = jnp.exp(cw_last - cw_h)
        at = a_h * jnp.exp(cw_h - lw[:, sl])
        bt = b_h * e_neg
        kt = k_h * e_neg
        rt = r_h * jnp.exp(cw_h)
        a_ab = jnp.where(strict, _bdot_nt(at, bt), 0.0)
        a_ak = jnp.where(strict, _bdot_nt(at, kt), 0.0)
        a_rb = jnp.where(causal, _bdot_nt(rt, bt), 0.0)
        a_rk = jnp.where(causal, _bdot_nt(rt, kt), 0.0)
        s = s_ref[0, j]
        rhs = _bdot_nt(at, s) + _bdot(a_ak, v_h)
        u = _hdot(_neumann_inverse(a_ab, eye_f, nsq), rhs)
        y = _bdot_nt(rt, s) + _bdot(a_rb, u) + _bdot(a_rk, v_h)
        s_ref[0, j] = s * jnp.exp(cw_last) + _bdot_tn(u, b_h * e_end) + _bdot_tn(v_h, k_h * e_end)

        mu = jnp.mean(y, axis=-1, keepdims=True)
        var = jnp.mean(jnp.square(y - mu), axis=-1, keepdims=True)
        yn = (y - mu) * lax.rsqrt(var + RWKV_GN_EPS) * lnw[:, sl] + lnb[:, sl]
        bonus = jnp.sum(r_h * k_h * rk[:, sl], axis=-1, keepdims=True) * v_h
        o_ref[:, sl] = (yn + bonus) * gate[:, sl]


def _rwkv_call(u_main, u_misc, init_main, init_misc, pr, s0, lyr, *, grp, D, hb):
    row0, B, Tp, L, t_real = grp
    H = D // RWKV_HEAD
    hb = min(hb, H)
    W = hb * RWKV_HEAD
    nc = Tp // L
    cm = u_misc.shape[1]
    rb = lambda b, c: (row0 + b * Tp) // L + c
    pb = lambda b, c: jnp.maximum((row0 + b * Tp + c * L) // SUBLANES - 1, 0)
    off = 7 * D // W
    seg = D // W
    cur = lambda s: pl.BlockSpec((L, W), lambda b, h, c: (rb(b, c), off + s * seg + h))
    prev = lambda s: pl.BlockSpec((SUBLANES, W), lambda b, h, c: (pb(b, c), off + s * seg + h))
    init = lambda s: pl.BlockSpec((1, SUBLANES, W), lambda b, h, c: (b, 0, s * seg + h))
    vec = lambda s: pl.BlockSpec((1, W), lambda b, h, c: (0, s * seg + h))
    full = lambda rows: pl.BlockSpec((rows, W), lambda b, h, c: (0, h))
    nsq = int(math.log2(L)) - 1
    return pl.pallas_call(
        functools.partial(_rwkv_kernel, L=L, t_real=t_real, hb=hb, nsq=nsq),
        grid=(B, H // hb, nc),
        in_specs=[cur(0), cur(1), cur(2), prev(0), prev(1), prev(2), init(0), init(1), init(2),
                  pl.BlockSpec((L, cm), lambda b, h, c: (rb(b, c), 0)),
                  pl.BlockSpec((SUBLANES, cm), lambda b, h, c: (pb(b, c), 0)),
                  pl.BlockSpec((1, SUBLANES, cm), lambda b, h, c: (b, 0, 0)),
                  vec(0), vec(1), vec(2),
                  pl.BlockSpec((1, cm), lambda b, h, c: (0, 0)),
                  vec(0), full(LANES), vec(0), full(LANES), full(cm - MISC_GL),
                  vec(0), vec(0), vec(0), vec(0), vec(0),
                  pl.BlockSpec((1, 1, hb, RWKV_HEAD, RWKV_HEAD), lambda b, h, c: (lyr, b, h, 0, 0))],
        out_specs=[pl.BlockSpec((L, W), lambda b, h, c: (b * nc + c, h)),
                   pl.BlockSpec((1, hb, RWKV_HEAD, RWKV_HEAD), lambda b, h, c: (b, h, 0, 0))],
        out_shape=[jax.ShapeDtypeStruct((B * Tp, D), F32),
                   jax.ShapeDtypeStruct((B, H, RWKV_HEAD, RWKV_HEAD), F32)],
        compiler_params=_params(("parallel", "parallel", "arbitrary")),
        name="rwkv_mixer",
    )(u_main, u_main, u_main, u_main, u_main, u_main, init_main, init_main, init_main,
      u_misc, u_misc, init_misc, pr["mu_main"], pr["mu_main"], pr["mu_main"], pr["mu_misc"],
      pr["w0"], pr["w2"], pr["a0"], pr["a2"], pr["g2"], pr["k_k"], pr["k_a"], pr["r_k"],
      pr["ln_w"], pr["ln_b"], s0)


def _merge_kernel(o0_ref, o1_ref, o2_ref, g0_ref, g1_ref, g2_ref, wb_ref, out_ref):
    acc = _sigmoid(g0_ref[...]) * _bdot(o0_ref[...], wb_ref[0])
    acc = acc + _sigmoid(g1_ref[...]) * _bdot(o1_ref[...], wb_ref[1])
    acc = acc + _sigmoid(g2_ref[...]) * _bdot(o2_ref[...], wb_ref[2])
    out_ref[...] = acc.astype(out_ref.dtype)


def _merge_call(o_gla, o_gdn, o_rwkv, u_main, w_branch, *, D, row0, tm, tn):
    n = o_gla.shape[0]
    gate_off = 10 * D // tn
    rb0 = row0 // tm
    o_spec = pl.BlockSpec((tm, D), lambda i, j: (i, 0))
    g_spec = lambda s: pl.BlockSpec((tm, tn), lambda i, j: (rb0 + i, gate_off + s * (D // tn) + j))
    return pl.pallas_call(
        _merge_kernel,
        grid=(n // tm, D // tn),
        in_specs=[o_spec, o_spec, o_spec, g_spec(0), g_spec(1), g_spec(2),
                  pl.BlockSpec((3, D, tn), lambda i, j: (0, 0, j))],
        out_specs=pl.BlockSpec((tm, tn), lambda i, j: (i, j)),
        out_shape=jax.ShapeDtypeStruct((n, D), BF16),
        compiler_params=_params(("parallel", "arbitrary")),
        name="branch_merge",
    )(o_gla, o_gdn, o_rwkv, u_main, u_main, u_main, w_branch)


def _layernorm(x, g, b):
    mu = jnp.mean(x, axis=-1, keepdims=True)
    var = jnp.mean(jnp.square(x - mu), axis=-1, keepdims=True)
    return (x - mu) * lax.rsqrt(var + LN_EPS) * g + b


def _outproj_kernel(m_ref, w_ref, x_ref, g_ref, b_ref, h_ref, *, alpha):
    y = alpha * x_ref[...] + jnp.dot(m_ref[...], w_ref[...], preferred_element_type=F32)
    h_ref[...] = _layernorm(y, g_ref[...], b_ref[...])


def _outproj_call(merged, w_out, x_pad, g, b, *, alpha, row0, tm):
    n, D = merged.shape
    rb0 = row0 // tm
    return pl.pallas_call(
        functools.partial(_outproj_kernel, alpha=alpha),
        grid=(n // tm,),
        in_specs=[pl.BlockSpec((tm, D), lambda i: (i, 0)),
                  pl.BlockSpec((D, D), lambda i: (0, 0)),
                  pl.BlockSpec((tm, D), lambda i: (rb0 + i, 0)),
                  pl.BlockSpec((1, D), lambda i: (0, 0)),
                  pl.BlockSpec((1, D), lambda i: (0, 0))],
        out_specs=pl.BlockSpec((tm, D), lambda i: (i, 0)),
        out_shape=jax.ShapeDtypeStruct((n, D), F32),
        compiler_params=_params(("parallel",)),
        name="out_proj_ln",
    )(merged, w_out, x_pad, g, b)


def _router_kernel(h_ref, wr_ref, bias_ref, ids_ref, wts_ref, *, n_exp):
    tm = h_ref.shape[0]
    logits = _bdot_nt(wr_ref[...], h_ref[...])[:n_exp]
    scores = _sigmoid(logits)
    sel = scores + bias_ref[...][:n_exp]
    gsz = n_exp // N_GROUPS
    neg = -jnp.inf

    def first_argmax(vals, iota, n):
        mx = jnp.max(vals, axis=0, keepdims=True)
        idx = jnp.min(jnp.where(vals == mx, iota, n), axis=0, keepdims=True)
        return mx, idx

    io_g = lax.broadcasted_iota(jnp.int32, (gsz, tm), 0)
    grp_rows = []
    for gi in range(N_GROUPS):
        blk = sel[gi * gsz:(gi + 1) * gsz]
        m1, i1 = first_argmax(blk, io_g, gsz)
        m2 = jnp.max(jnp.where(io_g == i1, neg, blk), axis=0, keepdims=True)
        grp_rows.append(m1 + m2)
    grp = jnp.concatenate(grp_rows, axis=0)
    io_n = lax.broadcasted_iota(jnp.int32, (N_GROUPS, tm), 0)
    gmask = jnp.zeros((N_GROUPS, tm), F32)
    for _ in range(TOPK_GROUPS):
        _, gi = first_argmax(grp, io_n, N_GROUPS)
        hit = io_n == gi
        gmask = jnp.where(hit, 1.0, gmask)
        grp = jnp.where(hit, neg, grp)
    io_e = lax.broadcasted_iota(jnp.int32, (n_exp, tm), 0)
    emask = jnp.concatenate(
        [jnp.broadcast_to(gmask[gi:gi + 1], (gsz, tm)) for gi in range(N_GROUPS)], axis=0)
    cand = jnp.where(emask > 0.5, sel, neg)
    ids, wts = [], []
    for _ in range(TOP_K):
        _, ei = first_argmax(cand, io_e, n_exp)
        hit = io_e == ei
        ids.append(ei)
        wts.append(jnp.sum(jnp.where(hit, scores, 0.0), axis=0, keepdims=True))
        cand = jnp.where(hit, neg, cand)
    w = jnp.concatenate(wts, axis=0)
    ids_ref[...] = jnp.concatenate(ids, axis=0)
    wts_ref[...] = w / (jnp.sum(w, axis=0, keepdims=True) + 1e-20) * ROUTED_SCALE


def _router_call(h, wr_t, bias_col, *, n_exp, tm):
    n, D = h.shape
    return pl.pallas_call(
        functools.partial(_router_kernel, n_exp=n_exp),
        grid=(n // tm,),
        in_specs=[pl.BlockSpec((tm, D), lambda i: (i, 0)),
                  pl.BlockSpec(wr_t.shape, lambda i: (0, 0)),
                  pl.BlockSpec(bias_col.shape, lambda i: (0, 0))],
        out_specs=[pl.BlockSpec((TOP_K, tm), lambda i: (0, i)),
                   pl.BlockSpec((TOP_K, tm), lambda i: (0, i))],
        out_shape=[jax.ShapeDtypeStruct((TOP_K, n), jnp.int32),
                   jax.ShapeDtypeStruct((TOP_K, n), F32)],
        compiler_params=_params(("parallel",)),
        name="router_topk",
    )(h, wr_t, bias_col)


def _moe_kernel(blk_e_ref, n_used_ref, src_ref, dst_ref, h_hbm, sw_ref, wgu_ref, wd_ref, y_hbm,
                xbuf, ybuf, sem_in, sem_out, *, tm):
    i = pl.program_id(0)

    def in_copy(r, src):
        return pltpu.make_async_copy(h_hbm.at[pl.ds(src, 1)], xbuf.at[pl.ds(r, 1)], sem_in)

    def out_copy(r, dst):
        return pltpu.make_async_copy(ybuf.at[pl.ds(r, 1)], y_hbm.at[pl.ds(dst, 1)], sem_out)

    @pl.when(i < n_used_ref[0])
    def _():
        def start_in(r, carry):
            in_copy(r, src_ref[0, 0, r]).start()
            return carry

        lax.fori_loop(0, tm, start_in, 0)

        def wait_in(r, carry):
            in_copy(r, 0).wait()
            return carry

        lax.fori_loop(0, tm, wait_in, 0)

        f = wd_ref.shape[1]
        gu = jnp.dot(xbuf[...].astype(BF16), wgu_ref[0], preferred_element_type=F32)
        gt = gu[:, :f]
        act = gt * _sigmoid(gt) * gu[:, f:]
        y = jnp.dot(act.astype(BF16), wd_ref[0], preferred_element_type=F32)
        ybuf[...] = y * sw_ref[...]

        def start_out(r, carry):
            dst = dst_ref[0, 0, r]

            @pl.when(dst >= 0)
            def _():
                out_copy(r, dst).start()

            return carry

        lax.fori_loop(0, tm, start_out, 0)

        def wait_out(r, carry):
            dst = dst_ref[0, 0, r]

            @pl.when(dst >= 0)
            def _():
                out_copy(r, dst).wait()

            return carry

        lax.fori_loop(0, tm, wait_out, 0)


def _moe_call(blk_e, n_used, src, dst, h, slot_w, w_gu, w_down, *, tm, nk):
    n_blocks = src.shape[0]
    n, D = h.shape
    f2 = w_gu.shape[2]
    f = w_down.shape[1]
    grid_spec = pltpu.PrefetchScalarGridSpec(
        num_scalar_prefetch=2,
        grid=(n_blocks,),
        in_specs=[pl.BlockSpec((1, 1, tm), lambda i, be, nu: (i, 0, 0), memory_space=pltpu.SMEM),
                  pl.BlockSpec((1, 1, tm), lambda i, be, nu: (i, 0, 0), memory_space=pltpu.SMEM),
                  pl.BlockSpec(memory_space=pl.ANY),
                  pl.BlockSpec((tm, 1), lambda i, be, nu: (i, 0)),
                  pl.BlockSpec((1, D, f2), lambda i, be, nu: (be[i], 0, 0)),
                  pl.BlockSpec((1, f, D), lambda i, be, nu: (be[i], 0, 0))],
        out_specs=pl.BlockSpec(memory_space=pl.ANY),
        scratch_shapes=[pltpu.VMEM((tm, D), F32), pltpu.VMEM((tm, D), F32),
                        pltpu.SemaphoreType.DMA(()), pltpu.SemaphoreType.DMA(())],
    )
    return pl.pallas_call(
        functools.partial(_moe_kernel, tm=tm),
        grid_spec=grid_spec,
        out_shape=jax.ShapeDtypeStruct((nk * n, D), F32),
        compiler_params=_params(("arbitrary",)),
        name="moe_experts",
    )(blk_e, n_used, src, dst, h, slot_w, w_gu, w_down)


def _combine_kernel(y_ref, h_ref, g_ref, b_ref, o_ref, *, alpha, nk):
    k = pl.program_id(1)

    @pl.when(k == 0)
    def _():
        o_ref[...] = alpha * h_ref[...]

    o_ref[...] += y_ref[...]

    @pl.when(k == nk - 1)
    def _():
        o_ref[...] = _layernorm(o_ref[...], g_ref[...], b_ref[...])


def _combine_call(y_slots, h, g, b, *, alpha, tm, nk):
    n, D = h.shape
    slab_blocks = n // tm
    return pl.pallas_call(
        functools.partial(_combine_kernel, alpha=alpha, nk=nk),
        grid=(n // tm, nk),
        in_specs=[pl.BlockSpec((tm, D), lambda i, k: (k * slab_blocks + i, 0)),
                  pl.BlockSpec((tm, D), lambda i, k: (i, 0)),
                  pl.BlockSpec((1, D), lambda i, k: (0, 0)),
                  pl.BlockSpec((1, D), lambda i, k: (0, 0))],
        out_specs=pl.BlockSpec((tm, D), lambda i, k: (i, 0)),
        out_shape=jax.ShapeDtypeStruct((n, D), F32),
        compiler_params=_params(("parallel", "arbitrary")),
        name="moe_combine_ln",
    )(y_slots, h, g, b)


def _dispatch_tables(ids, wts, *, n_exp_tot, tm, n_blocks):
    n, nk = ids.shape
    onehot = (ids[:, :, None] == jnp.arange(n_exp_tot, dtype=jnp.int32)).any(axis=1)
    mask = onehot.astype(jnp.int32)
    cum = jnp.cumsum(mask, axis=0) - mask
    pos = jnp.take_along_axis(cum, ids, axis=1)
    counts = mask.sum(axis=0)
    padded = (counts + tm - 1) // tm * tm
    pad_end = jnp.cumsum(padded)
    pad_start = pad_end - padded
    dest = pad_start[ids] + pos
    n_used = (pad_end[-1] // tm).astype(jnp.int32).reshape(1)
    p = n_blocks * tm
    slot_a = jnp.full((p,), -1, jnp.int32).at[dest.reshape(-1)].set(jnp.arange(n * nk, dtype=jnp.int32))
    used = slot_a >= 0
    a = jnp.maximum(slot_a, 0)
    src = jnp.where(used, a // nk, 0)
    dst = jnp.where(used, (a % nk) * n + a // nk, -1)
    slot_w = jnp.where(used, wts.reshape(-1)[a], 0.0)
    blk_start = jnp.arange(n_blocks, dtype=jnp.int32) * tm
    blk_e = jnp.minimum(jnp.searchsorted(pad_end, blk_start, side="right"), n_exp_tot - 1).astype(jnp.int32)
    return (blk_e, n_used, src.reshape(n_blocks, 1, tm), dst.reshape(n_blocks, 1, tm), slot_w.reshape(p, 1))


def _pad_rows(a, lanes):
    return jnp.pad(a, ((0, 0), (0, lanes - a.shape[1])))


def _layer(x, lyr, st, w, *, dims):
    D, Bp, T, Bs, Ts, Tsp = dims
    n_p = Bp * T
    n_s = Bs * Ts
    n = n_p + n_s
    depth = w["w_in"].shape[0]
    alpha = (2 * depth) ** 0.25
    Lp = math.gcd(T, CHUNK)
    grp_p = (0, Bp, T, Lp, T)
    grp_s = (n_p, Bs, Tsp, Tsp, Ts)
    H_gdn = D // GDN_HEAD
    lora = w["gla_w_lora"].shape[1]
    lane_b, lane_a = lora, lora + H_gdn
    wl_n, al_n, gl_n = w["rwkv_w2"].shape[1], w["rwkv_a2"].shape[1], w["rwkv_g2"].shape[1]
    assert lane_a + H_gdn <= LANES and wl_n <= LANES and al_n <= LANES and gl_n % LANES == 0
    assert T % Lp == 0 and n_p % Tsp == 0 and Ts >= GDN_CONV - 1 and Tsp % SUBLANES == 0

    wi = w["w_in"][lyr]
    o_glr = 3 * D
    o_gdn = o_glr + lora
    o_gb = o_gdn + 4 * D
    o_ga = o_gb + H_gdn
    o_rw = o_ga + H_gdn
    o_wl = o_rw + 3 * D
    o_al = o_wl + wl_n
    o_gl = o_al + al_n
    o_gate = o_gl + gl_n
    w_main = jnp.concatenate([wi[:, :3 * D], wi[:, o_gdn:o_gdn + 4 * D], wi[:, o_rw:o_rw + 3 * D],
                              wi[:, o_gate:o_gate + 3 * D]], axis=1).astype(BF16)
    w_misc = jnp.concatenate([
        _pad_rows(jnp.concatenate([wi[:, o_glr:o_glr + lora], wi[:, o_gb:o_gb + 2 * H_gdn]], axis=1), LANES),
        _pad_rows(wi[:, o_wl:o_wl + wl_n], LANES), _pad_rows(wi[:, o_al:o_al + al_n], LANES),
        wi[:, o_gl:o_gl + gl_n]], axis=1).astype(BF16)
    cm = w_misc.shape[1]

    xs_pad = jnp.pad(x[n_p:].reshape(Bs, Ts, D), ((0, 0), (0, Tsp - Ts), (0, 0))).reshape(Bs * Tsp, D)
    x_pad = jnp.concatenate([x[:n_p], xs_pad], axis=0)
    np_rows = x_pad.shape[0]
    x_bf = x_pad.astype(BF16)

    tm = _pick(np_rows, (1024, 512, 256, 128, 96, 64, 32, 16, 8))
    u_main = _matmul(x_bf, w_main, tm, _pick(w_main.shape[1], (1024, 512, 256, 128)))
    u_misc = _matmul(x_bf, w_misc, tm, cm)

    wl_ext = jnp.pad(w["gla_w_lora"][lyr], ((0, LANES - lora), (0, 0))).astype(BF16)
    b_lora = w["gla_b_lora"][lyr][None]
    gla_nw = w["gla_norm_w"][lyr][None]
    conv_w = w["gdn_conv_w"][lyr]
    gdn_prm = jnp.zeros((SUBLANES, LANES), F32)
    gdn_prm = gdn_prm.at[0, lane_a:lane_a + H_gdn].set(w["gdn_A_log"][lyr])
    gdn_prm = gdn_prm.at[1, lane_a:lane_a + H_gdn].set(w["gdn_dt_bias"][lyr])
    gdn_nw = w["gdn_norm_w"][lyr][None]
    mu = w["rwkv_mu"][lyr]

    def misc_layout(a):
        z = jnp.zeros(a.shape[:-1] + (LANES,), a.dtype)
        return jnp.concatenate([z, _pad_last(a[..., :wl_n]), _pad_last(a[..., wl_n:wl_n + al_n]),
                                a[..., wl_n + al_n:]], axis=-1)

    def _pad_last(a):
        return jnp.pad(a, [(0, 0)] * (a.ndim - 1) + [(0, LANES - a.shape[-1])])

    rw = {
        "mu_main": mu[None, :3 * D], "mu_misc": misc_layout(mu[None, 3 * D:]),
        "w0": w["rwkv_w0"][lyr][None],
        "w2": jnp.pad(w["rwkv_w2"][lyr], ((0, LANES - wl_n), (0, 0))).astype(BF16),
        "a0": w["rwkv_a0"][lyr][None],
        "a2": jnp.pad(w["rwkv_a2"][lyr], ((0, LANES - al_n), (0, 0))).astype(BF16),
        "g2": w["rwkv_g2"][lyr].astype(BF16),
        "k_k": w["rwkv_k_k"][lyr][None], "k_a": w["rwkv_k_a"][lyr][None],
        "r_k": w["rwkv_r_k"][lyr].reshape(1, D),
        "ln_w": w["rwkv_ln_w"][lyr][None], "ln_b": w["rwkv_ln_b"][lyr][None],
    }

    w_branch = w["w_branch"][lyr].astype(BF16)
    w_out = w["w_out"][lyr].astype(BF16)
    ln1_g, ln1_b = w["ln1_g"][lyr][None], w["ln1_b"][lyr][None]
    h_groups = []
    new_states = []
    for grp, s in ((grp_p, st["p"]), (grp_s, st["s"])):
        row0, B, Tp, L, t_real = grp
        l_idx = lyr if s["layered"] else 0
        conv_init = jnp.pad(s["conv"], ((0, 0), (SUBLANES - (GDN_CONV - 1), 0), (0, 0)))
        shift = s["shift"]
        init_main = jnp.pad(shift[:, None, :3 * D], ((0, 0), (SUBLANES - 1, 0), (0, 0)))
        init_misc = jnp.pad(misc_layout(shift[:, None, 3 * D:]), ((0, 0), (SUBLANES - 1, 0), (0, 0)))
        o_gla, s_gla = _gla_call(u_main, u_misc, wl_ext, b_lora, gla_nw, s["gla"], l_idx, grp=grp, D=D)
        o_gdn, s_gdn = _gdn_call(u_main, u_misc, conv_init, conv_w, gdn_prm, gdn_nw, s["gdn"], l_idx,
                                 grp=grp, D=D, lane_b=lane_b, lane_a=lane_a, hb=4)
        o_rwkv, s_rwkv = _rwkv_call(u_main, u_misc, init_main, init_misc, rw, s["rwkv"], l_idx,
                                    grp=grp, D=D, hb=8)
        tm2 = _pick(math.gcd(B * Tp, row0) if row0 else B * Tp, (512, 256, 128, 64, 32, 16))
        merged = _merge_call(o_gla, o_gdn, o_rwkv, u_main, w_branch, D=D, row0=row0, tm=tm2,
                             tn=_pick(D, (512, 256, 128)))
        h_groups.append(_outproj_call(merged, w_out, x_pad, ln1_g, ln1_b, alpha=alpha, row0=row0, tm=tm2))
        u3 = u_main[row0:row0 + B * Tp].reshape(B, Tp, -1)
        m3 = u_misc[row0:row0 + B * Tp].reshape(B, Tp, -1)
        new_conv = u3[:, t_real - (GDN_CONV - 1):t_real, 3 * D:6 * D]
        new_shift = jnp.concatenate([u3[:, t_real - 1, 7 * D:10 * D], m3[:, t_real - 1, MISC_WL:MISC_WL + wl_n],
                                     m3[:, t_real - 1, MISC_AL:MISC_AL + al_n], m3[:, t_real - 1, MISC_GL:]], axis=-1)
        new_states.append((s_gla, s_gdn, new_conv, s_rwkv, new_shift))

    h = jnp.concatenate([h_groups[0], h_groups[1].reshape(Bs, Tsp, D)[:, :Ts].reshape(n_s, D)], axis=0)

    n_exp = w["w_router"].shape[2]
    wr_t = jnp.pad(w["w_router"][lyr].T, ((0, LANES - n_exp), (0, 0))).astype(BF16)
    bias_col = jnp.pad(w["router_bias"][lyr], (0, LANES - n_exp))[:, None]
    tmr = _pick(n, (512, 256, 128, 64, 32, 16, 8))
    ids_t, wts_t = _router_call(h, wr_t, bias_col, n_exp=n_exp, tm=tmr)
    ids = jnp.concatenate([ids_t.T, jnp.full((n, 1), n_exp, jnp.int32)], axis=1)
    wts = jnp.concatenate([wts_t.T, jnp.ones((n, 1), F32)], axis=1)
    nk = TOP_K + 1
    tme = _pick(n, (256, 128, 64, 32, 16, 8))
    n_blocks = -(-n * nk // tme) + n_exp + 1
    blk_e, n_used, src, dst, slot_w = _dispatch_tables(ids, wts, n_exp_tot=n_exp + 1, tm=tme, n_blocks=n_blocks)
    w_gu = jnp.concatenate([w["w_exp_gu"][lyr], w["w_sh_gu"][lyr][None]], axis=0).astype(BF16)
    w_dn = jnp.concatenate([w["w_exp_down"][lyr], w["w_sh_down"][lyr][None]], axis=0).astype(BF16)
    y_slots = _moe_call(blk_e, n_used, src, dst, h, slot_w, w_gu, w_dn, tm=tme, nk=nk)
    x_out = _combine_call(y_slots, h, w["ln2_g"][lyr][None], w["ln2_b"][lyr][None], alpha=alpha, tm=tme, nk=nk)
    return x_out, new_states


def kernel(x_prompt, x_sample, state_gla, state_gdn, state_gdn_conv, state_rwkv, state_rwkv_shift, w_in, gla_w_lora, gla_b_lora, gla_norm_w, gdn_conv_w, gdn_A_log, gdn_dt_bias, gdn_norm_w, rwkv_mu, rwkv_w0, rwkv_w2, rwkv_a0, rwkv_a2, rwkv_g2, rwkv_k_k, rwkv_k_a, rwkv_r_k, rwkv_ln_w, rwkv_ln_b, w_branch, w_out, ln1_g, ln1_b, w_router, router_bias, w_exp_gu, w_exp_down, w_sh_gu, w_sh_down, ln2_g, ln2_b):
    w = dict(w_in=w_in, gla_w_lora=gla_w_lora, gla_b_lora=gla_b_lora, gla_norm_w=gla_norm_w,
             gdn_conv_w=gdn_conv_w, gdn_A_log=gdn_A_log, gdn_dt_bias=gdn_dt_bias, gdn_norm_w=gdn_norm_w,
             rwkv_mu=rwkv_mu, rwkv_w0=rwkv_w0, rwkv_w2=rwkv_w2, rwkv_a0=rwkv_a0, rwkv_a2=rwkv_a2,
             rwkv_g2=rwkv_g2, rwkv_k_k=rwkv_k_k, rwkv_k_a=rwkv_k_a, rwkv_r_k=rwkv_r_k, rwkv_ln_w=rwkv_ln_w,
             rwkv_ln_b=rwkv_ln_b, w_branch=w_branch, w_out=w_out, ln1_g=ln1_g, ln1_b=ln1_b,
             w_router=w_router, router_bias=router_bias, w_exp_gu=w_exp_gu, w_exp_down=w_exp_down,
             w_sh_gu=w_sh_gu, w_sh_down=w_sh_down, ln2_g=ln2_g, ln2_b=ln2_b)
    Bp, T, D = x_prompt.shape
    Bs, Ts, _ = x_sample.shape
    depth = w_in.shape[0]
    Tsp = -(-Ts // SUBLANES) * SUBLANES
    dims = (D, Bp, T, Bs, Ts, Tsp)
    conv_cols = state_gdn_conv.shape[-1]
    shift_cols = state_rwkv_shift.shape[-1]
    zero_p = dict(layered=False,
                  gla=jnp.zeros((1, Bp) + state_gla.shape[2:], F32),
                  gdn=jnp.zeros((1, Bp) + state_gdn.shape[2:], F32),
                  rwkv=jnp.zeros((1, Bp) + state_rwkv.shape[2:], F32),
                  conv=jnp.zeros((Bp, GDN_CONV - 1, conv_cols), F32),
                  shift=jnp.zeros((Bp, shift_cols), F32))
    x = jnp.concatenate([x_prompt.reshape(Bp * T, D), x_sample.reshape(Bs * Ts, D)], axis=0)
    new_p, new_s = [], []
    for lyr in range(depth):
        st = {"p": zero_p,
              "s": dict(layered=True, gla=state_gla, gdn=state_gdn, rwkv=state_rwkv,
                        conv=state_gdn_conv[lyr], shift=state_rwkv_shift[lyr])}
        x, (sp, ss) = _layer(x, lyr, st, w, dims=dims)
        new_p.append(sp)
        new_s.append(ss)
    outs_p = tuple(jnp.stack([s[i] for s in new_p]) for i in range(5))
    outs_s = tuple(jnp.stack([s[i] for s in new_s]) for i in range(5))
    n_p = Bp * T
    return (x[:n_p].reshape(Bp, T, D), x[n_p:].reshape(Bs, Ts, D)) + outs_p + outs_s
```

```python
import functools
import math

import jax
import jax.numpy as jnp
from jax import lax
from jax.experimental import pallas as pl
from jax.experimental.pallas import tpu as pltpu

F32 = jnp.float32
BF16 = jnp.bfloat16
HIGHEST = lax.Precision.HIGHEST

GLA_HEADS = 4
GLA_GATE_TAU = 16.0
GDN_HEAD = 128
GDN_CONV = 4
RWKV_HEAD = 64
RWKV_GN_EPS = 64e-5
CHUNK = 64
TOP_K = 8
N_GROUPS = 8
TOPK_GROUPS = 4
ROUTED_SCALE = 2.5
LN_EPS = 1e-5
RMS_EPS = 1e-6
L2_EPS = 1e-6

LANES = 128
SUBLANES = 8
VMEM_LIMIT = 56 * 1024 * 1024

MISC_WL = 128
MISC_AL = 256
MISC_GL = 384


def _pick(n, cands):
    for c in cands:
        if n % c == 0:
            return c
    raise ValueError(f"no tile for {n}")


def _params(sem):
    return pltpu.CompilerParams(dimension_semantics=sem, vmem_limit_bytes=VMEM_LIMIT)


def _bdot(a, b):
    return jnp.dot(a.astype(BF16), b.astype(BF16), preferred_element_type=F32)


def _bdot_nt(a, b):
    return lax.dot_general(a.astype(BF16), b.astype(BF16), (((1,), (1,)), ((), ())), preferred_element_type=F32)


def _bdot_tn(a, b):
    return lax.dot_general(a.astype(BF16), b.astype(BF16), (((0,), (0,)), ((), ())), preferred_element_type=F32)


def _hdot(a, b):
    return jnp.dot(a, b, precision=HIGHEST, preferred_element_type=F32)


def _softplus(x):
    return jnp.maximum(x, 0.0) + jnp.log1p(jnp.exp(-jnp.abs(x)))


def _sigmoid(x):
    return 1.0 / (1.0 + jnp.exp(-x))


def _tri_masks(L):
    ri = lax.broadcasted_iota(jnp.int32, (L, L), 0)
    ci = lax.broadcasted_iota(jnp.int32, (L, L), 1)
    return ri >= ci, ri > ci, ri == ci


def _split2(x):
    hi = x.astype(BF16)
    lo = (x - hi.astype(F32)).astype(BF16)
    return hi, lo


def _dot3(a, b):
    ah, al = _split2(a)
    bh, bl = _split2(b)
    d = lambda x, y: jnp.dot(x, y, preferred_element_type=F32)
    return d(ah, bh) + (d(ah, bl) + d(al, bh))


def _cumsum_rows(tril_bf, x):
    hi = x.astype(BF16)
    r1 = x - hi.astype(F32)
    mid = r1.astype(BF16)
    lo = (r1 - mid.astype(F32)).astype(BF16)
    d = lambda y: jnp.dot(tril_bf, y, preferred_element_type=F32)
    return d(hi) + (d(mid) + d(lo))


def _unit_lower_solve(xs, ws, n_stage):
    n = ws[0].shape[1]
    ts = list(xs)
    ws = list(ws)
    for s in range(n_stage):
        last = s == n_stage - 1
        outs = [_dot3(t, w if last else jnp.concatenate([w, t], axis=1)) for t, w in zip(ts, ws)]
        ws = [w + o[:, :n] for w, o in zip(ws, outs)]
        if not last:
            ts = [o[:, n:] for o in outs]
    return ws


def _mm_kernel(x_ref, w_ref, o_ref):
    o_ref[...] = jnp.dot(x_ref[...], w_ref[...], preferred_element_type=F32)


def _matmul(x, w, tm, tn):
    n, k = x.shape
    c = w.shape[1]
    return pl.pallas_call(
        _mm_kernel,
        grid=(n // tm, c // tn),
        in_specs=[pl.BlockSpec((tm, k), lambda i, j: (i, 0)),
                  pl.BlockSpec((k, tn), lambda i, j: (0, j))],
        out_specs=pl.BlockSpec((tm, tn), lambda i, j: (i, j)),
        out_shape=jax.ShapeDtypeStruct((n, c), F32),
        compiler_params=_params(("parallel", "parallel")),
        name="in_proj",
    )(x, w)


def _gla_kernel(q_ref, k_ref, v_ref, og_ref, misc_ref, wl_ref, bl_ref, nw_ref, s0_ref, o_ref, s_ref,
                *, L, t_real):
    c = pl.program_id(2)

    @pl.when(c == 0)
    def _():
        s_ref[0, 0] = s0_ref[0, 0, 0]

    dk = q_ref.shape[1]
    row = lax.broadcasted_iota(jnp.int32, (L, 1), 0)
    valid = (c * L + row) < t_real
    causal, _, _ = _tri_masks(L)

    z = _bdot(misc_ref[...], wl_ref[...]) + bl_ref[...]
    log_a = jnp.where(valid, -_softplus(-z) / GLA_GATE_TAU, 0.0)
    q = q_ref[...]
    k = jnp.where(valid, k_ref[...], 0.0)
    v = jnp.where(valid, v_ref[...], 0.0)

    g = _cumsum_rows(causal.astype(BF16), log_a)
    g_last = g[L - 1:L, :]
    qg = q * (dk ** -0.5) * jnp.exp(g)
    kg = k * jnp.exp(-g)
    k_end = k * jnp.exp(g_last - g)
    a = jnp.where(causal, _bdot_nt(qg, kg), 0.0)
    s = s_ref[0, 0]
    o = _bdot(a, v) + _bdot(qg, s)

    ri = lax.broadcasted_iota(jnp.int32, (dk, dk), 0)
    ci = lax.broadcasted_iota(jnp.int32, (dk, dk), 1)
    dec_col = jnp.sum(jnp.where(ri == ci, jnp.broadcast_to(jnp.exp(g_last), (dk, dk)), 0.0), axis=1, keepdims=True)
    s_ref[0, 0] = dec_col * s + _bdot_tn(k_end, v)

    rms = o * lax.rsqrt(jnp.mean(o * o, axis=-1, keepdims=True) + RMS_EPS) * nw_ref[...]
    og = og_ref[...]
    o_ref[...] = rms * (og * _sigmoid(og))


def _gla_call(u_main, u_misc, wl_ext, b_lora, norm_w, s0, lyr, *, grp, D):
    row0, B, Tp, L, t_real = grp
    H = GLA_HEADS
    dv = D // H
    dk = dv // 2
    nc = Tp // L
    rb = lambda b, c: (row0 + b * Tp) // L + c
    return pl.pallas_call(
        functools.partial(_gla_kernel, L=L, t_real=t_real),
        grid=(B, H, nc),
        in_specs=[
            pl.BlockSpec((L, dk), lambda b, h, c: (rb(b, c), h)),
            pl.BlockSpec((L, dk), lambda b, h, c: (rb(b, c), (D // 2) // dk + h)),
            pl.BlockSpec((L, dv), lambda b, h, c: (rb(b, c), D // dv + h)),
            pl.BlockSpec((L, dv), lambda b, h, c: (rb(b, c), 2 * D // dv + h)),
            pl.BlockSpec((L, LANES), lambda b, h, c: (rb(b, c), 0)),
            pl.BlockSpec((LANES, dk), lambda b, h, c: (0, h)),
            pl.BlockSpec((1, dk), lambda b, h, c: (0, h)),
            pl.BlockSpec((1, dv), lambda b, h, c: (0, 0)),
            pl.BlockSpec((1, 1, 1, dk, dv), lambda b, h, c: (lyr, b, h, 0, 0)),
        ],
        out_specs=[
            pl.BlockSpec((L, dv), lambda b, h, c: (b * nc + c, h)),
            pl.BlockSpec((1, 1, dk, dv), lambda b, h, c: (b, h, 0, 0)),
        ],
        out_shape=[jax.ShapeDtypeStruct((B * Tp, D), F32),
                   jax.ShapeDtypeStruct((B, H, dk, dv), F32)],
        compiler_params=_params(("parallel", "parallel", "arbitrary")),
        name="gla_mixer",
    )(u_main, u_main, u_main, u_main, u_misc, wl_ext, b_lora, norm_w, s0)


def _gdn_kernel(q_ref, k_ref, v_ref, pq_ref, pk_ref, pv_ref, iq_ref, ik_ref, iv_ref,
                cq_ref, ck_ref, cv_ref, gz_ref, misc_ref, prm_ref, nw_ref, s0_ref, o_ref, s_ref,
                *, L, t_real, hb, lane_b, lane_a, n_stage):
    c = pl.program_id(2)
    hg0 = pl.program_id(1) * hb

    @pl.when(c == 0)
    def _():
        s_ref[0] = s0_ref[0, 0]

    row = lax.broadcasted_iota(jnp.int32, (L, 1), 0)
    valid = (c * L + row) < t_real
    causal, strict, _ = _tri_masks(L)

    def conv(cur_ref, prev_ref, init_ref, w_ref):
        cur = cur_ref[...]
        prev = jnp.where(c == 0, init_ref[0], prev_ref[...])
        ext = jnp.concatenate([prev, cur], axis=0)
        w = w_ref[...]
        y = cur * w[GDN_CONV - 1:GDN_CONV]
        for j in range(1, GDN_CONV):
            y = y + pltpu.roll(ext, j, 0)[SUBLANES:] * w[GDN_CONV - 1 - j:GDN_CONV - j]
        return y * _sigmoid(y)

    qc = conv(q_ref, pq_ref, iq_ref, cq_ref)
    kc = conv(k_ref, pk_ref, ik_ref, ck_ref)
    vc = conv(v_ref, pv_ref, iv_ref, cv_ref)

    misc = misc_ref[...]
    prm = prm_ref[...]
    beta_all = jnp.where(valid, _sigmoid(misc), 0.0)
    g_all = jnp.where(valid, -jnp.exp(prm[0:1]) * _softplus(misc + prm[1:2]), 0.0)
    gc_all = _cumsum_rows(causal.astype(BF16), g_all)
    gc_all_t = gc_all.T
    lane = lax.broadcasted_iota(jnp.int32, (L, LANES), 1)
    sub = lax.broadcasted_iota(jnp.int32, (LANES, L), 0)
    gz = gz_ref[...]
    nw = nw_ref[...]
    hd = GDN_HEAD

    heads = range(hb)
    sls = [slice(j * hd, (j + 1) * hd) for j in heads]
    ks, kq, decay, rhs, q_dec, k_end, dec_last = [], [], [], [], [], [], []
    for j in heads:
        hg = hg0 + j
        sl = sls[j]
        beta = jnp.sum(jnp.where(lane == lane_b + hg, beta_all, 0.0), axis=1, keepdims=True)
        gc = jnp.sum(jnp.where(lane == lane_a + hg, gc_all, 0.0), axis=1, keepdims=True)
        gr = jnp.sum(jnp.where(sub == lane_a + hg, gc_all_t, 0.0), axis=0, keepdims=True)
        gc_last = gc[L - 1:L]
        q = qc[:, sl]
        k = kc[:, sl]
        q = q * lax.rsqrt(jnp.sum(q * q, axis=-1, keepdims=True) + L2_EPS) * (hd ** -0.5)
        k = k * lax.rsqrt(jnp.sum(k * k, axis=-1, keepdims=True) + L2_EPS)
        e_gc = jnp.exp(gc)
        k_beta = k * beta
        ks.append(k)
        kq.append(jnp.concatenate([k_beta, q], axis=0))
        decay.append(jnp.where(causal, jnp.exp(jnp.where(causal, gc - gr, 0.0)), 0.0))
        rhs.append(jnp.concatenate([vc[:, sl] * beta, k_beta * e_gc], axis=1))
        q_dec.append(q * e_gc)
        k_end.append(k * jnp.exp(gc_last - gc))
        dec_last.append(jnp.exp(gc_last))
    sc = [_bdot_nt(kq[j], ks[j]) for j in heads]
    xs = [-jnp.where(strict, sc[j][:L] * decay[j], 0.0) for j in heads]
    sol = _unit_lower_solve(xs, rhs, n_stage)
    s_old = [s_ref[0, j] for j in heads]
    ws = [_bdot(jnp.concatenate([sol[j][:, hd:], q_dec[j]], axis=0), s_old[j]) for j in heads]
    v_new = [sol[j][:, :hd] - ws[j][:L] for j in heads]
    os_ = [ws[j][L:] + _bdot(sc[j][L:] * decay[j], v_new[j]) for j in heads]
    for j in heads:
        s_ref[0, j] = dec_last[j] * s_old[j] + _bdot_tn(k_end[j], v_new[j])
    for j in heads:
        o = os_[j]
        rms = o * lax.rsqrt(jnp.mean(o * o, axis=-1, keepdims=True) + RMS_EPS) * nw
        z = gz[:, sls[j]]
        o_ref[:, sls[j]] = rms * (z * _sigmoid(z))


def _gdn_call(u_main, u_misc, conv_init, conv_w, prm, norm_w, s0, lyr, *, grp, D, lane_b, lane_a, hb):
    row0, B, Tp, L, t_real = grp
    H = D // GDN_HEAD
    hb = min(hb, H)
    W = hb * GDN_HEAD
    nc = Tp // L
    rb = lambda b, c: (row0 + b * Tp) // L + c
    pb = lambda b, c: jnp.maximum((row0 + b * Tp + c * L) // SUBLANES - 1, 0)
    off = 3 * D // W
    seg = D // W
    cur = lambda s: pl.BlockSpec((L, W), lambda b, h, c: (rb(b, c), off + s * seg + h))
    prev = lambda s: pl.BlockSpec((SUBLANES, W), lambda b, h, c: (pb(b, c), off + s * seg + h))
    init = lambda s: pl.BlockSpec((1, SUBLANES, W), lambda b, h, c: (b, 0, s * seg + h))
    cw = lambda s: pl.BlockSpec((GDN_CONV, W), lambda b, h, c: (0, s * seg + h))
    n_stage = int(math.log2(L))
    assert 1 << n_stage == L
    return pl.pallas_call(
        functools.partial(_gdn_kernel, L=L, t_real=t_real, hb=hb, lane_b=lane_b, lane_a=lane_a, n_stage=n_stage),
        grid=(B, H // hb, nc),
        in_specs=[cur(0), cur(1), cur(2), prev(0), prev(1), prev(2), init(0), init(1), init(2),
                  cw(0), cw(1), cw(2),
                  pl.BlockSpec((L, W), lambda b, h, c: (rb(b, c), 6 * D // W + h)),
                  pl.BlockSpec((L, LANES), lambda b, h, c: (rb(b, c), 0)),
                  pl.BlockSpec((SUBLANES, LANES), lambda b, h, c: (0, 0)),
                  pl.BlockSpec((1, GDN_HEAD), lambda b, h, c: (0, 0)),
                  pl.BlockSpec((1, 1, hb, GDN_HEAD, GDN_HEAD), lambda b, h, c: (lyr, b, h, 0, 0))],
        out_specs=[pl.BlockSpec((L, W), lambda b, h, c: (b * nc + c, h)),
                   pl.BlockSpec((1, hb, GDN_HEAD, GDN_HEAD), lambda b, h, c: (b, h, 0, 0))],
        out_shape=[jax.ShapeDtypeStruct((B * Tp, D), F32),
                   jax.ShapeDtypeStruct((B, H, GDN_HEAD, GDN_HEAD), F32)],
        compiler_params=_params(("parallel", "parallel", "arbitrary")),
        name="gdn_mixer",
    )(u_main, u_main, u_main, u_main, u_main, u_main, conv_init, conv_init, conv_init,
      conv_w, conv_w, conv_w, u_main, u_misc, prm, norm_w, s0)


def _rwkv_kernel(r_ref, k_ref, v_ref, pr_ref, pk_ref, pv_ref, ir_ref, ik_ref, iv_ref,
                 misc_ref, pmisc_ref, imisc_ref, mur_ref, muk_ref, muv_ref, mum_ref,
                 w0_ref, w2_ref, a0_ref, a2_ref, g2_ref, kk_ref, ka_ref, rk_ref, lnw_ref, lnb_ref, s0_ref,
                 o_ref, s_ref, *, L, t_real, hb, n_stage):
    c = pl.program_id(2)

    @pl.when(c == 0)
    def _():
        s_ref[0] = s0_ref[0, 0]

    row = lax.broadcasted_iota(jnp.int32, (L, 1), 0)
    valid = (c * L + row) < t_real
    causal, _, _ = _tri_masks(L)

    def mix(cur_ref, prev_ref, init_ref, mu_ref):
        cur = cur_ref[...]
        prev8 = jnp.where(c == 0, init_ref[0], prev_ref[...])
        ext = jnp.concatenate([prev8, cur], axis=0)
        prev = pltpu.roll(ext, 1, 0)[SUBLANES:]
        return cur + (prev - cur) * mu_ref[...]

    r = mix(r_ref, pr_ref, ir_ref, mur_ref)
    k = mix(k_ref, pk_ref, ik_ref, muk_ref)
    v = mix(v_ref, pv_ref, iv_ref, muv_ref)
    m = mix(misc_ref, pmisc_ref, imisc_ref, mum_ref)
    wl = m[:, MISC_WL:MISC_AL]
    al = m[:, MISC_AL:MISC_GL]
    gl = m[:, MISC_GL:]

    w_log = -_softplus(-(w0_ref[...] + _bdot(jnp.tanh(wl), w2_ref[...]))) - 0.5
    lw = jnp.where(valid, -jnp.exp(w_log), 0.0)
    a_sig = _sigmoid(a0_ref[...] + _bdot(al, a2_ref[...]))
    gate = _bdot(_sigmoid(gl), g2_ref[...])
    kk = k * kk_ref[...]
    k2 = jnp.where(valid, k * (1.0 + (a_sig - 1.0) * ka_ref[...]), 0.0)
    rk = rk_ref[...]
    lnw = lnw_ref[...]
    lnb = lnb_ref[...]
    cw = _cumsum_rows(causal.astype(BF16), lw)
    hd = RWKV_HEAD
    ri2 = lax.broadcasted_iota(jnp.int32, (2 * L, 2 * L), 0)
    ci2 = lax.broadcasted_iota(jnp.int32, (2 * L, 2 * L), 1)
    mask2 = ((ri2 & (L - 1)) - (ci2 & (L - 1)) + jnp.where(ri2 >= L, 1, 0)) > 0

    heads = range(hb)
    sls = [slice(j * hd, (j + 1) * hd) for j in heads]
    ar, bk, end, w_last = [], [], [], []
    for sl in sls:
        cw_h = cw[:, sl]
        cw_last = cw_h[L - 1:L]
        kk_h = kk[:, sl]
        kk_h = jnp.where(valid, kk_h * lax.rsqrt(jnp.sum(kk_h * kk_h, axis=-1, keepdims=True) + L2_EPS), 0.0)
        b_h = kk_h * a_sig[:, sl]
        e_neg = jnp.exp(-cw_h)
        e_end = jnp.exp(cw_last - cw_h)
        ar.append(jnp.concatenate([-kk_h * jnp.exp(cw_h - lw[:, sl]), r[:, sl] * jnp.exp(cw_h)], axis=0))
        bk.append(jnp.concatenate([b_h * e_neg, k2[:, sl] * e_neg], axis=0))
        end.append(jnp.concatenate([b_h * e_end, k2[:, sl] * e_end], axis=0))
        w_last.append(jnp.exp(cw_last))
    s_old = [s_ref[0, j] for j in heads]
    g = [jnp.where(mask2, _bdot_nt(ar[j], bk[j]), 0.0) for j in heads]
    ars = [_bdot_nt(ar[j], s_old[j]) for j in heads]
    akv = [_bdot(g[j][:, L:], v[:, sls[j]]) for j in heads]
    u = _unit_lower_solve([g[j][:L, :L] for j in heads], [ars[j][:L] + akv[j][:L] for j in heads], n_stage)
    ys = [ars[j][L:] + akv[j][L:] + _bdot(g[j][L:, :L], u[j]) for j in heads]
    for j in heads:
        s_ref[0, j] = s_old[j] * w_last[j] + _bdot_tn(jnp.concatenate([u[j], v[:, sls[j]]], axis=0), end[j])
    for j in heads:
        sl = sls[j]
        y = ys[j]
        mu = jnp.mean(y, axis=-1, keepdims=True)
        var = jnp.mean(jnp.square(y - mu), axis=-1, keepdims=True)
        yn = (y - mu) * lax.rsqrt(var + RWKV_GN_EPS) * lnw[:, sl] + lnb[:, sl]
        bonus = jnp.sum(r[:, sl] * k2[:, sl] * rk[:, sl], axis=-1, keepdims=True) * v[:, sl]
        o_ref[:, sl] = (yn + bonus) * gate[:, sl]


def _rwkv_call(u_main, u_misc, init_main, init_misc, pr, s0, lyr, *, grp, D, hb):
    row0, B, Tp, L, t_real = grp
    H = D // RWKV_HEAD
    hb = min(hb, H)
    W = hb * RWKV_HEAD
    nc = Tp // L
    cm = u_misc.shape[1]
    rb = lambda b, c: (row0 + b * Tp) // L + c
    pb = lambda b, c: jnp.maximum((row0 + b * Tp + c * L) // SUBLANES - 1, 0)
    off = 7 * D // W
    seg = D // W
    cur = lambda s: pl.BlockSpec((L, W), lambda b, h, c: (rb(b, c), off + s * seg + h))
    prev = lambda s: pl.BlockSpec((SUBLANES, W), lambda b, h, c: (pb(b, c), off + s * seg + h))
    init = lambda s: pl.BlockSpec((1, SUBLANES, W), lambda b, h, c: (b, 0, s * seg + h))
    vec = lambda s: pl.BlockSpec((1, W), lambda b, h, c: (0, s * seg + h))
    full = lambda rows: pl.BlockSpec((rows, W), lambda b, h, c: (0, h))
    n_stage = int(math.log2(L))
    assert 1 << n_stage == L
    return pl.pallas_call(
        functools.partial(_rwkv_kernel, L=L, t_real=t_real, hb=hb, n_stage=n_stage),
        grid=(B, H // hb, nc),
        in_specs=[cur(0), cur(1), cur(2), prev(0), prev(1), prev(2), init(0), init(1), init(2),
                  pl.BlockSpec((L, cm), lambda b, h, c: (rb(b, c), 0)),
                  pl.BlockSpec((SUBLANES, cm), lambda b, h, c: (pb(b, c), 0)),
                  pl.BlockSpec((1, SUBLANES, cm), lambda b, h, c: (b, 0, 0)),
                  vec(0), vec(1), vec(2),
                  pl.BlockSpec((1, cm), lambda b, h, c: (0, 0)),
                  vec(0), full(LANES), vec(0), full(LANES), full(cm - MISC_GL),
                  vec(0), vec(0), vec(0), vec(0), vec(0),
                  pl.BlockSpec((1, 1, hb, RWKV_HEAD, RWKV_HEAD), lambda b, h, c: (lyr, b, h, 0, 0))],
        out_specs=[pl.BlockSpec((L, W), lambda b, h, c: (b * nc + c, h)),
                   pl.BlockSpec((1, hb, RWKV_HEAD, RWKV_HEAD), lambda b, h, c: (b, h, 0, 0))],
        out_shape=[jax.ShapeDtypeStruct((B * Tp, D), F32),
                   jax.ShapeDtypeStruct((B, H, RWKV_HEAD, RWKV_HEAD), F32)],
        compiler_params=_params(("parallel", "parallel", "arbitrary")),
        name="rwkv_mixer",
    )(u_main, u_main, u_main, u_main, u_main, u_main, init_main, init_main, init_main,
      u_misc, u_misc, init_misc, pr["mu_main"], pr["mu_main"], pr["mu_main"], pr["mu_misc"],
      pr["w0"], pr["w2"], pr["a0"], pr["a2"], pr["g2"], pr["k_k"], pr["k_a"], pr["r_k"],
      pr["ln_w"], pr["ln_b"], s0)


def _merge_kernel(o0_ref, o1_ref, o2_ref, g0_ref, g1_ref, g2_ref, wb_ref, out_ref):
    acc = _sigmoid(g0_ref[...]) * _bdot(o0_ref[...], wb_ref[0])
    acc = acc + _sigmoid(g1_ref[...]) * _bdot(o1_ref[...], wb_ref[1])
    acc = acc + _sigmoid(g2_ref[...]) * _bdot(o2_ref[...], wb_ref[2])
    out_ref[...] = acc.astype(out_ref.dtype)


def _merge_call(o_gla, o_gdn, o_rwkv, u_main, w_branch, *, D, row0, tm, tn):
    n = o_gla.shape[0]
    gate_off = 10 * D // tn
    rb0 = row0 // tm
    o_spec = pl.BlockSpec((tm, D), lambda i, j: (i, 0))
    g_spec = lambda s: pl.BlockSpec((tm, tn), lambda i, j: (rb0 + i, gate_off + s * (D // tn) + j))
    return pl.pallas_call(
        _merge_kernel,
        grid=(n // tm, D // tn),
        in_specs=[o_spec, o_spec, o_spec, g_spec(0), g_spec(1), g_spec(2),
                  pl.BlockSpec((3, D, tn), lambda i, j: (0, 0, j))],
        out_specs=pl.BlockSpec((tm, tn), lambda i, j: (i, j)),
        out_shape=jax.ShapeDtypeStruct((n, D), BF16),
        compiler_params=_params(("parallel", "arbitrary")),
        name="branch_merge",
    )(o_gla, o_gdn, o_rwkv, u_main, u_main, u_main, w_branch)


def _layernorm(x, g, b):
    mu = jnp.mean(x, axis=-1, keepdims=True)
    var = jnp.mean(jnp.square(x - mu), axis=-1, keepdims=True)
    return (x - mu) * lax.rsqrt(var + LN_EPS) * g + b


def _outproj_kernel(m_ref, w_ref, x_ref, g_ref, b_ref, h_ref, *, alpha):
    y = alpha * x_ref[...] + jnp.dot(m_ref[...], w_ref[...], preferred_element_type=F32)
    h_ref[...] = _layernorm(y, g_ref[...], b_ref[...])


def _outproj_call(merged, w_out, x_pad, g, b, *, alpha, row0, tm):
    n, D = merged.shape
    rb0 = row0 // tm
    return pl.pallas_call(
        functools.partial(_outproj_kernel, alpha=alpha),
        grid=(n // tm,),
        in_specs=[pl.BlockSpec((tm, D), lambda i: (i, 0)),
                  pl.BlockSpec((D, D), lambda i: (0, 0)),
                  pl.BlockSpec((tm, D), lambda i: (rb0 + i, 0)),
                  pl.BlockSpec((1, D), lambda i: (0, 0)),
                  pl.BlockSpec((1, D), lambda i: (0, 0))],
        out_specs=pl.BlockSpec((tm, D), lambda i: (i, 0)),
        out_shape=jax.ShapeDtypeStruct((n, D), F32),
        compiler_params=_params(("parallel",)),
        name="out_proj_ln",
    )(merged, w_out, x_pad, g, b)


def _router_kernel(h_ref, wr_ref, bias_ref, ids_ref, wts_ref, *, n_exp):
    tm = h_ref.shape[0]
    logits = _bdot_nt(wr_ref[...], h_ref[...])[:n_exp]
    scores = _sigmoid(logits)
    sel = scores + bias_ref[...][:n_exp]
    gsz = n_exp // N_GROUPS
    neg = -jnp.inf

    def first_argmax(vals, iota, n):
        mx = jnp.max(vals, axis=0, keepdims=True)
        idx = jnp.min(jnp.where(vals == mx, iota, n), axis=0, keepdims=True)
        return mx, idx

    io_g = lax.broadcasted_iota(jnp.int32, (gsz, tm), 0)
    grp_rows = []
    for gi in range(N_GROUPS):
        blk = sel[gi * gsz:(gi + 1) * gsz]
        m1, i1 = first_argmax(blk, io_g, gsz)
        m2 = jnp.max(jnp.where(io_g == i1, neg, blk), axis=0, keepdims=True)
        grp_rows.append(m1 + m2)
    grp = jnp.concatenate(grp_rows, axis=0)
    io_n = lax.broadcasted_iota(jnp.int32, (N_GROUPS, tm), 0)
    gmask = jnp.zeros((N_GROUPS, tm), F32)
    for _ in range(TOPK_GROUPS):
        _, gi = first_argmax(grp, io_n, N_GROUPS)
        hit = io_n == gi
        gmask = jnp.where(hit, 1.0, gmask)
        grp = jnp.where(hit, neg, grp)
    io_e = lax.broadcasted_iota(jnp.int32, (n_exp, tm), 0)
    emask = jnp.concatenate(
        [jnp.broadcast_to(gmask[gi:gi + 1], (gsz, tm)) for gi in range(N_GROUPS)], axis=0)
    cand = jnp.where(emask > 0.5, sel, neg)
    ids, wts = [], []
    for _ in range(TOP_K):
        _, ei = first_argmax(cand, io_e, n_exp)
        hit = io_e == ei
        ids.append(ei)
        wts.append(jnp.sum(jnp.where(hit, scores, 0.0), axis=0, keepdims=True))
        cand = jnp.where(hit, neg, cand)
    w = jnp.concatenate(wts, axis=0)
    ids_ref[...] = jnp.concatenate(ids, axis=0)
    wts_ref[...] = w / (jnp.sum(w, axis=0, keepdims=True) + 1e-20) * ROUTED_SCALE


def _router_call(h, wr_t, bias_col, *, n_exp, tm):
    n, D = h.shape
    return pl.pallas_call(
        functools.partial(_router_kernel, n_exp=n_exp),
        grid=(n // tm,),
        in_specs=[pl.BlockSpec((tm, D), lambda i: (i, 0)),
                  pl.BlockSpec(wr_t.shape, lambda i: (0, 0)),
                  pl.BlockSpec(bias_col.shape, lambda i: (0, 0))],
        out_specs=[pl.BlockSpec((TOP_K, tm), lambda i: (0, i)),
                   pl.BlockSpec((TOP_K, tm), lambda i: (0, i))],
        out_shape=[jax.ShapeDtypeStruct((TOP_K, n), jnp.int32),
                   jax.ShapeDtypeStruct((TOP_K, n), F32)],
        compiler_params=_params(("parallel",)),
        name="router_topk",
    )(h, wr_t, bias_col)


DMA_UNROLL = 8
GDN_HEADS_PER_STEP = (16, 8)
RWKV_HEADS_PER_STEP = (32, 16)


def _swiglu(x_bf, w_gu, w_down):
    f = w_down.shape[0]
    gu = jnp.dot(x_bf, w_gu, preferred_element_type=F32)
    gt = gu[:, :f]
    act = gt * _sigmoid(gt) * gu[:, f:]
    return jnp.dot(act.astype(BF16), w_down, preferred_element_type=F32)


def _moe_kernel(blk_e_ref, n_used_ref, src_ref, nsrc_ref, h_hbm, sw_ref, wgu_ref, wd_ref, y_ref,
                xbuf, sem_in, *, tm):
    i = pl.program_id(0)
    n_used = n_used_ref[0]
    slot = i % 2

    def gather(idx_ref, buf_slot):
        def body(r, carry):
            pltpu.make_async_copy(h_hbm.at[pl.ds(idx_ref[0, 0, r], 1)], xbuf.at[buf_slot, pl.ds(r, 1)],
                                  sem_in.at[buf_slot]).start()
            return carry
        lax.fori_loop(0, tm, body, 0, unroll=DMA_UNROLL)

    @pl.when(jnp.logical_and(i == 0, n_used > 0))
    def _():
        gather(src_ref, 0)

    @pl.when(i + 1 < n_used)
    def _():
        gather(nsrc_ref, 1 - slot)

    @pl.when(i < n_used)
    def _():
        pltpu.make_async_copy(xbuf.at[slot], xbuf.at[slot], sem_in.at[slot]).wait()
        y_ref[...] = _swiglu(xbuf[slot].astype(BF16), wgu_ref[0], wd_ref[0]) * sw_ref[...]

    @pl.when(i >= n_used)
    def _():
        y_ref[...] = jnp.zeros_like(y_ref)


def _moe_call(blk_e, n_used, src, h, slot_w, w_gu, w_down, *, tm):
    n_blocks = src.shape[0]
    n, D = h.shape
    f2 = w_gu.shape[2]
    f = w_down.shape[1]
    grid_spec = pltpu.PrefetchScalarGridSpec(
        num_scalar_prefetch=2,
        grid=(n_blocks,),
        in_specs=[pl.BlockSpec((1, 1, tm), lambda i, be, nu: (i, 0, 0), memory_space=pltpu.SMEM),
                  pl.BlockSpec((1, 1, tm), lambda i, be, nu: (jnp.minimum(i + 1, n_blocks - 1), 0, 0),
                               memory_space=pltpu.SMEM),
                  pl.BlockSpec(memory_space=pl.ANY),
                  pl.BlockSpec((tm, 1), lambda i, be, nu: (i, 0)),
                  pl.BlockSpec((1, D, f2), lambda i, be, nu: (be[i], 0, 0)),
                  pl.BlockSpec((1, f, D), lambda i, be, nu: (be[i], 0, 0))],
        out_specs=pl.BlockSpec((tm, D), lambda i, be, nu: (i, 0)),
        scratch_shapes=[pltpu.VMEM((2, tm, D), F32), pltpu.SemaphoreType.DMA((2,))],
    )
    return pl.pallas_call(
        functools.partial(_moe_kernel, tm=tm),
        grid_spec=grid_spec,
        out_shape=jax.ShapeDtypeStruct((n_blocks * tm, D), F32),
        compiler_params=_params(("arbitrary",)),
        name="moe_experts",
    )(blk_e, n_used, src, src, h, slot_w, w_gu, w_down)


def _combine_kernel(dest_ref, y_hbm, h_ref, wgu_ref, wd_ref, g_ref, b_ref, o_ref, gbuf, sem, *, alpha, nk, tm):
    for k in range(nk):
        def body(r, carry, k=k):
            pltpu.make_async_copy(y_hbm.at[pl.ds(dest_ref[0, 0, k * tm + r], 1)], gbuf.at[k, pl.ds(r, 1)],
                                  sem).start()
            return carry
        lax.fori_loop(0, tm, body, 0, unroll=DMA_UNROLL)

    h = h_ref[...]
    acc = alpha * h + _swiglu(h.astype(BF16), wgu_ref[...], wd_ref[...])
    pltpu.make_async_copy(gbuf, gbuf, sem).wait()
    for k in range(nk):
        acc = acc + gbuf[k]
    o_ref[...] = _layernorm(acc, g_ref[...], b_ref[...])


def _combine_call(dest, y_sorted, h, w_sh_gu, w_sh_down, g, b, *, alpha, tm, nk):
    n, D = h.shape
    f2 = w_sh_gu.shape[1]
    f = w_sh_down.shape[0]
    return pl.pallas_call(
        functools.partial(_combine_kernel, alpha=alpha, nk=nk, tm=tm),
        grid=(n // tm,),
        in_specs=[pl.BlockSpec((1, 1, nk * tm), lambda i: (i, 0, 0), memory_space=pltpu.SMEM),
                  pl.BlockSpec(memory_space=pl.ANY),
                  pl.BlockSpec((tm, D), lambda i: (i, 0)),
                  pl.BlockSpec((D, f2), lambda i: (0, 0)),
                  pl.BlockSpec((f, D), lambda i: (0, 0)),
                  pl.BlockSpec((1, D), lambda i: (0, 0)),
                  pl.BlockSpec((1, D), lambda i: (0, 0))],
        out_specs=pl.BlockSpec((tm, D), lambda i: (i, 0)),
        out_shape=jax.ShapeDtypeStruct((n, D), F32),
        scratch_shapes=[pltpu.VMEM((nk, tm, D), F32), pltpu.SemaphoreType.DMA(())],
        compiler_params=_params(("arbitrary",)),
        name="moe_combine_ln",
    )(dest, y_sorted, h, w_sh_gu, w_sh_down, g, b)


def _dispatch_tables(ids, wts, *, n_exp_tot, tm, n_blocks):
    n, nk = ids.shape
    onehot = (ids[:, :, None] == jnp.arange(n_exp_tot, dtype=jnp.int32)).any(axis=1)
    mask = onehot.astype(jnp.int32)
    cum = jnp.cumsum(mask, axis=0) - mask
    pos = jnp.take_along_axis(cum, ids, axis=1)
    counts = mask.sum(axis=0)
    padded = (counts + tm - 1) // tm * tm
    pad_end = jnp.cumsum(padded)
    pad_start = pad_end - padded
    dest = pad_start[ids] + pos
    n_used = (pad_end[-1] // tm).astype(jnp.int32).reshape(1)
    p = n_blocks * tm
    slot_a = jnp.full((p,), -1, jnp.int32).at[dest.reshape(-1)].set(jnp.arange(n * nk, dtype=jnp.int32))
    used = slot_a >= 0
    a = jnp.maximum(slot_a, 0)
    src = jnp.where(used, a // nk, 0)
    slot_w = jnp.where(used, wts.reshape(-1)[a], 0.0)
    blk_start = jnp.arange(n_blocks, dtype=jnp.int32) * tm
    blk_e = jnp.minimum(jnp.searchsorted(pad_end, blk_start, side="right"), n_exp_tot - 1).astype(jnp.int32)
    dest_t = dest.reshape(n // tm, tm, nk).transpose(0, 2, 1).reshape(n // tm, 1, nk * tm)
    return blk_e, n_used, src.reshape(n_blocks, 1, tm), slot_w.reshape(p, 1), dest_t


def _pad_rows(a, lanes):
    return jnp.pad(a, ((0, 0), (0, lanes - a.shape[1])))


def _layer(x, lyr, st, w, *, dims):
    D, Bp, T, Bs, Ts, Tsp = dims
    n_p = Bp * T
    n_s = Bs * Ts
    n = n_p + n_s
    depth = w["w_in"].shape[0]
    alpha = (2 * depth) ** 0.25
    Lp = math.gcd(T, CHUNK)
    grp_p = (0, Bp, T, Lp, T)
    grp_s = (n_p, Bs, Tsp, Tsp, Ts)
    H_gdn = D // GDN_HEAD
    lora = w["gla_w_lora"].shape[1]
    lane_b, lane_a = lora, lora + H_gdn
    wl_n, al_n, gl_n = w["rwkv_w2"].shape[1], w["rwkv_a2"].shape[1], w["rwkv_g2"].shape[1]
    assert lane_a + H_gdn <= LANES and wl_n <= LANES and al_n <= LANES and gl_n % LANES == 0
    assert T % Lp == 0 and n_p % Tsp == 0 and Ts >= GDN_CONV - 1 and Tsp % SUBLANES == 0

    wi = w["w_in"][lyr]
    o_glr = 3 * D
    o_gdn = o_glr + lora
    o_gb = o_gdn + 4 * D
    o_ga = o_gb + H_gdn
    o_rw = o_ga + H_gdn
    o_wl = o_rw + 3 * D
    o_al = o_wl + wl_n
    o_gl = o_al + al_n
    o_gate = o_gl + gl_n
    w_main = jnp.concatenate([wi[:, :3 * D], wi[:, o_gdn:o_gdn + 4 * D], wi[:, o_rw:o_rw + 3 * D],
                              wi[:, o_gate:o_gate + 3 * D]], axis=1).astype(BF16)
    w_misc = jnp.concatenate([
        _pad_rows(jnp.concatenate([wi[:, o_glr:o_glr + lora], wi[:, o_gb:o_gb + 2 * H_gdn]], axis=1), LANES),
        _pad_rows(wi[:, o_wl:o_wl + wl_n], LANES), _pad_rows(wi[:, o_al:o_al + al_n], LANES),
        wi[:, o_gl:o_gl + gl_n]], axis=1).astype(BF16)
    cm = w_misc.shape[1]

    xs_pad = jnp.pad(x[n_p:].reshape(Bs, Ts, D), ((0, 0), (0, Tsp - Ts), (0, 0))).reshape(Bs * Tsp, D)
    x_pad = jnp.concatenate([x[:n_p], xs_pad], axis=0)
    np_rows = x_pad.shape[0]
    x_bf = x_pad.astype(BF16)

    tm = _pick(np_rows, (1024, 512, 256, 128, 96, 64, 32, 16, 8))
    u_main = _matmul(x_bf, w_main, tm, _pick(w_main.shape[1], (1024, 512, 256, 128)))
    u_misc = _matmul(x_bf, w_misc, tm, cm)

    wl_ext = jnp.pad(w["gla_w_lora"][lyr], ((0, LANES - lora), (0, 0))).astype(BF16)
    b_lora = w["gla_b_lora"][lyr][None]
    gla_nw = w["gla_norm_w"][lyr][None]
    conv_w = w["gdn_conv_w"][lyr]
    gdn_prm = jnp.zeros((SUBLANES, LANES), F32)
    gdn_prm = gdn_prm.at[0, lane_a:lane_a + H_gdn].set(w["gdn_A_log"][lyr])
    gdn_prm = gdn_prm.at[1, lane_a:lane_a + H_gdn].set(w["gdn_dt_bias"][lyr])
    gdn_nw = w["gdn_norm_w"][lyr][None]
    mu = w["rwkv_mu"][lyr]

    def misc_layout(a):
        z = jnp.zeros(a.shape[:-1] + (LANES,), a.dtype)
        return jnp.concatenate([z, _pad_last(a[..., :wl_n]), _pad_last(a[..., wl_n:wl_n + al_n]),
                                a[..., wl_n + al_n:]], axis=-1)

    def _pad_last(a):
        return jnp.pad(a, [(0, 0)] * (a.ndim - 1) + [(0, LANES - a.shape[-1])])

    rw = {
        "mu_main": mu[None, :3 * D], "mu_misc": misc_layout(mu[None, 3 * D:]),
        "w0": w["rwkv_w0"][lyr][None],
        "w2": jnp.pad(w["rwkv_w2"][lyr], ((0, LANES - wl_n), (0, 0))).astype(BF16),
        "a0": w["rwkv_a0"][lyr][None],
        "a2": jnp.pad(w["rwkv_a2"][lyr], ((0, LANES - al_n), (0, 0))).astype(BF16),
        "g2": w["rwkv_g2"][lyr].astype(BF16),
        "k_k": w["rwkv_k_k"][lyr][None], "k_a": w["rwkv_k_a"][lyr][None],
        "r_k": w["rwkv_r_k"][lyr].reshape(1, D),
        "ln_w": w["rwkv_ln_w"][lyr][None], "ln_b": w["rwkv_ln_b"][lyr][None],
    }

    w_branch = w["w_branch"][lyr].astype(BF16)
    w_out = w["w_out"][lyr].astype(BF16)
    ln1_g, ln1_b = w["ln1_g"][lyr][None], w["ln1_b"][lyr][None]
    h_groups = []
    new_states = []
    for grp, s in ((grp_p, st["p"]), (grp_s, st["s"])):
        row0, B, Tp, L, t_real = grp
        l_idx = lyr if s["layered"] else 0
        conv_init = jnp.pad(s["conv"], ((0, 0), (SUBLANES - (GDN_CONV - 1), 0), (0, 0)))
        shift = s["shift"]
        init_main = jnp.pad(shift[:, None, :3 * D], ((0, 0), (SUBLANES - 1, 0), (0, 0)))
        init_misc = jnp.pad(misc_layout(shift[:, None, 3 * D:]), ((0, 0), (SUBLANES - 1, 0), (0, 0)))
        o_gla, s_gla = _gla_call(u_main, u_misc, wl_ext, b_lora, gla_nw, s["gla"], l_idx, grp=grp, D=D)
        o_gdn, s_gdn = _gdn_call(u_main, u_misc, conv_init, conv_w, gdn_prm, gdn_nw, s["gdn"], l_idx,
                                 grp=grp, D=D, lane_b=lane_b, lane_a=lane_a, hb=GDN_HEADS_PER_STEP[L >= CHUNK])
        o_rwkv, s_rwkv = _rwkv_call(u_main, u_misc, init_main, init_misc, rw, s["rwkv"], l_idx,
                                    grp=grp, D=D, hb=RWKV_HEADS_PER_STEP[L >= CHUNK])
        tm2 = _pick(math.gcd(B * Tp, row0) if row0 else B * Tp, (512, 256, 128, 64, 32, 16))
        merged = _merge_call(o_gla, o_gdn, o_rwkv, u_main, w_branch, D=D, row0=row0, tm=tm2,
                             tn=_pick(D, (512, 256, 128)))
        h_groups.append(_outproj_call(merged, w_out, x_pad, ln1_g, ln1_b, alpha=alpha, row0=row0, tm=tm2))
        u3 = u_main[row0:row0 + B * Tp].reshape(B, Tp, -1)
        m3 = u_misc[row0:row0 + B * Tp].reshape(B, Tp, -1)
        new_conv = u3[:, t_real - (GDN_CONV - 1):t_real, 3 * D:6 * D]
        new_shift = jnp.concatenate([u3[:, t_real - 1, 7 * D:10 * D], m3[:, t_real - 1, MISC_WL:MISC_WL + wl_n],
                                     m3[:, t_real - 1, MISC_AL:MISC_AL + al_n], m3[:, t_real - 1, MISC_GL:]], axis=-1)
        new_states.append((s_gla, s_gdn, new_conv, s_rwkv, new_shift))

    h = jnp.concatenate([h_groups[0], h_groups[1].reshape(Bs, Tsp, D)[:, :Ts].reshape(n_s, D)], axis=0)

    n_exp = w["w_router"].shape[2]
    wr_t = jnp.pad(w["w_router"][lyr].T, ((0, LANES - n_exp), (0, 0))).astype(BF16)
    bias_col = jnp.pad(w["router_bias"][lyr], (0, LANES - n_exp))[:, None]
    tmr = _pick(n, (512, 256, 128, 64, 32, 16, 8))
    ids_t, wts_t = _router_call(h, wr_t, bias_col, n_exp=n_exp, tm=tmr)
    ids, wts = ids_t.T, wts_t.T
    tme = _pick(n, (256, 128, 64, 32, 16, 8))
    n_blocks = -(-n * TOP_K // tme) + n_exp
    blk_e, n_used, src, slot_w, dest_t = _dispatch_tables(ids, wts, n_exp_tot=n_exp, tm=tme, n_blocks=n_blocks)
    y_sorted = _moe_call(blk_e, n_used, src, h, slot_w, w["w_exp_gu"][lyr].astype(BF16),
                         w["w_exp_down"][lyr].astype(BF16), tm=tme)
    x_out = _combine_call(dest_t, y_sorted, h, w["w_sh_gu"][lyr].astype(BF16), w["w_sh_down"][lyr].astype(BF16),
                          w["ln2_g"][lyr][None], w["ln2_b"][lyr][None], alpha=alpha, tm=tme, nk=TOP_K)
    return x_out, new_states


def kernel(x_prompt, x_sample, state_gla, state_gdn, state_gdn_conv, state_rwkv, state_rwkv_shift, w_in, gla_w_lora, gla_b_lora, gla_norm_w, gdn_conv_w, gdn_A_log, gdn_dt_bias, gdn_norm_w, rwkv_mu, rwkv_w0, rwkv_w2, rwkv_a0, rwkv_a2, rwkv_g2, rwkv_k_k, rwkv_k_a, rwkv_r_k, rwkv_ln_w, rwkv_ln_b, w_branch, w_out, ln1_g, ln1_b, w_router, router_bias, w_exp_gu, w_exp_down, w_sh_gu, w_sh_down, ln2_g, ln2_b):
    w = dict(w_in=w_in, gla_w_lora=gla_w_lora, gla_b_lora=gla_b_lora, gla_norm_w=gla_norm_w,
             gdn_conv_w=gdn_conv_w, gdn_A_log=gdn_A_log, gdn_dt_bias=gdn_dt_bias, gdn_norm_w=gdn_norm_w,
             rwkv_mu=rwkv_mu, rwkv_w0=rwkv_w0, rwkv_w2=rwkv_w2, rwkv_a0=rwkv_a0, rwkv_a2=rwkv_a2,
             rwkv_g2=rwkv_g2, rwkv_k_k=rwkv_k_k, rwkv_k_a=rwkv_k_a, rwkv_r_k=rwkv_r_k, rwkv_ln_w=rwkv_ln_w,
             rwkv_ln_b=rwkv_ln_b, w_branch=w_branch, w_out=w_out, ln1_g=ln1_g, ln1_b=ln1_b,
             w_router=w_router, router_bias=router_bias, w_exp_gu=w_exp_gu, w_exp_down=w_exp_down,
             w_sh_gu=w_sh_gu, w_sh_down=w_sh_down, ln2_g=ln2_g, ln2_b=ln2_b)
    Bp, T, D = x_prompt.shape
    Bs, Ts, _ = x_sample.shape
    depth = w_in.shape[0]
    Tsp = -(-Ts // SUBLANES) * SUBLANES
    dims = (D, Bp, T, Bs, Ts, Tsp)
    conv_cols = state_gdn_conv.shape[-1]
    shift_cols = state_rwkv_shift.shape[-1]
    zero_p = dict(layered=False,
                  gla=jnp.zeros((1, Bp) + state_gla.shape[2:], F32),
                  gdn=jnp.zeros((1, Bp) + state_gdn.shape[2:], F32),
                  rwkv=jnp.zeros((1, Bp) + state_rwkv.shape[2:], F32),
                  conv=jnp.zeros((Bp, GDN_CONV - 1, conv_cols), F32),
                  shift=jnp.zeros((Bp, shift_cols), F32))
    x = jnp.concatenate([x_prompt.reshape(Bp * T, D), x_sample.reshape(Bs * Ts, D)], axis=0)
    new_p, new_s = [], []
    for lyr in range(depth):
        st = {"p": zero_p,
              "s": dict(layered=True, gla=state_gla, gdn=state_gdn, rwkv=state_rwkv,
                        conv=state_gdn_conv[lyr], shift=state_rwkv_shift[lyr])}
        x, (sp, ss) = _layer(x, lyr, st, w, dims=dims)
        new_p.append(sp)
        new_s.append(ss)
    outs_p = tuple(jnp.stack([s[i] for s in new_p]) for i in range(5))
    outs_s = tuple(jnp.stack([s[i] for s in new_s]) for i in range(5))
    n_p = Bp * T
    return (x[:n_p].reshape(Bp, T, D), x[n_p:].reshape(Bs, Ts, D)) + outs_p + outs_s
```

```python
import functools
import math

import jax
import jax.numpy as jnp
import numpy as np
from jax import lax
from jax.experimental import pallas as pl
from jax.experimental.pallas import tpu as pltpu

F32 = jnp.float32
BF16 = jnp.bfloat16
HIGHEST = lax.Precision.HIGHEST

GLA_HEADS = 4
GLA_GATE_TAU = 16.0
GDN_HEAD = 128
GDN_CONV = 4
RWKV_HEAD = 64
RWKV_GN_EPS = 64e-5
CHUNK = 64
TOP_K = 8
N_GROUPS = 8
TOPK_GROUPS = 4
ROUTED_SCALE = 2.5
LN_EPS = 1e-5
RMS_EPS = 1e-6
L2_EPS = 1e-6

LANES = 128
SUBLANES = 8
VMEM_LIMIT = 56 * 1024 * 1024

MISC_WL = 128
MISC_AL = 256
MISC_GL = 384


def _pick(n, cands):
    for c in cands:
        if n % c == 0:
            return c
    raise ValueError(f"no tile for {n}")


def _params(sem):
    return pltpu.CompilerParams(dimension_semantics=sem, vmem_limit_bytes=VMEM_LIMIT)


def _bdot(a, b):
    return jnp.dot(a.astype(BF16), b.astype(BF16), preferred_element_type=F32)


def _bdot_nt(a, b):
    return lax.dot_general(a.astype(BF16), b.astype(BF16), (((1,), (1,)), ((), ())), preferred_element_type=F32)


def _bdot_tn(a, b):
    return lax.dot_general(a.astype(BF16), b.astype(BF16), (((0,), (0,)), ((), ())), preferred_element_type=F32)


def _hdot(a, b):
    return jnp.dot(a, b, precision=HIGHEST, preferred_element_type=F32)


def _softplus(x):
    return jnp.maximum(x, 0.0) + jnp.log1p(jnp.exp(-jnp.abs(x)))


def _sigmoid(x):
    return 1.0 / (1.0 + jnp.exp(-x))


def _tri_masks(L):
    ri = lax.broadcasted_iota(jnp.int32, (L, L), 0)
    ci = lax.broadcasted_iota(jnp.int32, (L, L), 1)
    return ri >= ci, ri > ci, ri == ci


def _split2(x):
    hi = x.astype(BF16)
    lo = (x - hi.astype(F32)).astype(BF16)
    return hi, lo


def _dot3(a, b):
    ah, al = _split2(a)
    bh, bl = _split2(b)
    d = lambda x, y: jnp.dot(x, y, preferred_element_type=F32)
    return d(ah, bh) + (d(ah, bl) + d(al, bh))


def _cumsum_rows(tril_bf, x):
    hi = x.astype(BF16)
    r1 = x - hi.astype(F32)
    mid = r1.astype(BF16)
    lo = (r1 - mid.astype(F32)).astype(BF16)
    d = lambda y: jnp.dot(tril_bf, y, preferred_element_type=F32)
    return d(hi) + (d(mid) + d(lo))


def _unit_lower_solve(xs, ws, n_stage):
    n = ws[0].shape[1]
    ts = list(xs)
    ws = list(ws)
    for s in range(n_stage):
        last = s == n_stage - 1
        outs = [_dot3(t, w if last else jnp.concatenate([w, t], axis=1)) for t, w in zip(ts, ws)]
        ws = [w + o[:, :n] for w, o in zip(ws, outs)]
        if not last:
            ts = [o[:, n:] for o in outs]
    return ws


def _mm_kernel(x_ref, w_ref, o_ref):
    o_ref[...] = jnp.dot(x_ref[...], w_ref[...], preferred_element_type=F32)


def _matmul(x, w, tm, tn):
    n, k = x.shape
    c = w.shape[1]
    return pl.pallas_call(
        _mm_kernel,
        grid=(n // tm, c // tn),
        in_specs=[pl.BlockSpec((tm, k), lambda i, j: (i, 0)),
                  pl.BlockSpec((k, tn), lambda i, j: (0, j))],
        out_specs=pl.BlockSpec((tm, tn), lambda i, j: (i, j)),
        out_shape=jax.ShapeDtypeStruct((n, c), F32),
        compiler_params=_params(("parallel", "parallel")),
        name="in_proj",
    )(x, w)


def _gla_kernel(q_ref, k_ref, v_ref, og_ref, misc_ref, wl_ref, bl_ref, nw_ref, s0_ref, acc_ref, o_ref, s_ref,
                *, L, t_real, hb):
    del acc_ref
    c = pl.program_id(2)

    @pl.when(c == 0)
    def _():
        s_ref[0, 0] = s0_ref[0, 0]

    dk = q_ref.shape[1] // hb
    dv = v_ref.shape[1] // hb
    row = lax.broadcasted_iota(jnp.int32, (L, 1), 0)
    valid = (c * L + row) < t_real
    causal, _, _ = _tri_masks(L)

    z = _bdot(misc_ref[...], wl_ref[...]) + bl_ref[...]
    log_a = jnp.where(valid, -_softplus(-z) / GLA_GATE_TAU, 0.0)
    q = q_ref[...]
    k = jnp.where(valid, k_ref[...], 0.0)
    v = jnp.where(valid, v_ref[...], 0.0)

    g = _cumsum_rows(causal.astype(BF16), log_a)
    g_last = g[L - 1:L, :]
    qg = q * (dk ** -0.5) * jnp.exp(g)
    kg = k * jnp.exp(-g)
    k_end = k * jnp.exp(g_last - g)
    dec = jnp.exp(g_last)
    ri = lax.broadcasted_iota(jnp.int32, (dk, dk), 0)
    ci = lax.broadcasted_iota(jnp.int32, (dk, dk), 1)
    nw = nw_ref[...]
    og = og_ref[...]

    heads = range(hb)
    ks = [slice(j * dk, (j + 1) * dk) for j in heads]
    vs = [slice(j * dv, (j + 1) * dv) for j in heads]
    a = [jnp.where(causal, _bdot_nt(qg[:, ks[j]], kg[:, ks[j]]), 0.0) for j in heads]
    s_old = [s_ref[0, 0, j] for j in heads]
    os_ = [_bdot(a[j], v[:, vs[j]]) + _bdot(qg[:, ks[j]], s_old[j]) for j in heads]
    for j in heads:
        dec_col = jnp.sum(jnp.where(ri == ci, jnp.broadcast_to(dec[:, ks[j]], (dk, dk)), 0.0), axis=1, keepdims=True)
        s_ref[0, 0, j] = dec_col * s_old[j] + _bdot_tn(k_end[:, ks[j]], v[:, vs[j]])
    for j in heads:
        o = os_[j]
        rms = o * lax.rsqrt(jnp.mean(o * o, axis=-1, keepdims=True) + RMS_EPS) * nw
        z_g = og[:, vs[j]]
        o_ref[:, vs[j]] = rms * (z_g * _sigmoid(z_g))


def _gla_call(u_main, u_misc, wl_ext, b_lora, norm_w, s0, lyr, acc, lyr_out, *, grp, D, hb):
    row0, B, Tp, L, t_real = grp
    H = GLA_HEADS
    dv = D // H
    dk = dv // 2
    nc = Tp // L
    wk, wv = hb * dk, hb * dv
    rb = lambda b, c: (row0 + b * Tp) // L + c
    return pl.pallas_call(
        functools.partial(_gla_kernel, L=L, t_real=t_real, hb=hb),
        grid=(B, H // hb, nc),
        in_specs=[
            pl.BlockSpec((L, wk), lambda b, h, c: (rb(b, c), h)),
            pl.BlockSpec((L, wk), lambda b, h, c: (rb(b, c), (D // 2) // wk + h)),
            pl.BlockSpec((L, wv), lambda b, h, c: (rb(b, c), D // wv + h)),
            pl.BlockSpec((L, wv), lambda b, h, c: (rb(b, c), 2 * D // wv + h)),
            pl.BlockSpec((L, LANES), lambda b, h, c: (rb(b, c), 0)),
            pl.BlockSpec((LANES, wk), lambda b, h, c: (0, h)),
            pl.BlockSpec((1, wk), lambda b, h, c: (0, h)),
            pl.BlockSpec((1, dv), lambda b, h, c: (0, 0)),
            pl.BlockSpec((1, 1, hb, dk, dv), lambda b, h, c: (lyr, b, h, 0, 0)),
            pl.BlockSpec(memory_space=pl.ANY),
        ],
        out_specs=[
            pl.BlockSpec((L, wv), lambda b, h, c: (b * nc + c, h)),
            pl.BlockSpec((1, 1, hb, dk, dv), lambda b, h, c: (lyr_out, b, h, 0, 0)),
        ],
        out_shape=[jax.ShapeDtypeStruct((B * Tp, D), F32),
                   jax.ShapeDtypeStruct(acc.shape, F32)],
        input_output_aliases={9: 1},
        compiler_params=_params(("parallel", "parallel", "arbitrary")),
        name="gla_mixer",
    )(u_main, u_main, u_main, u_main, u_misc, wl_ext, b_lora, norm_w, s0, acc)


def _gdn_kernel(q_ref, k_ref, v_ref, pq_ref, pk_ref, pv_ref, iq_ref, ik_ref, iv_ref,
                cq_ref, ck_ref, cv_ref, gz_ref, misc_ref, prm_ref, nw_ref, s0_ref, acc_ref, o_ref, s_ref,
                *, L, t_real, hb, lane_b, lane_a, n_stage):
    del acc_ref
    c = pl.program_id(2)
    hg0 = pl.program_id(1) * hb

    @pl.when(c == 0)
    def _():
        s_ref[0, 0] = s0_ref[0, 0]

    row = lax.broadcasted_iota(jnp.int32, (L, 1), 0)
    valid = (c * L + row) < t_real
    causal, strict, _ = _tri_masks(L)

    def conv(cur_ref, prev_ref, init_ref, w_ref):
        cur = cur_ref[...]
        prev = jnp.where(c == 0, init_ref[0], prev_ref[...])
        ext = jnp.concatenate([prev, cur], axis=0)
        w = w_ref[...]
        y = cur * w[GDN_CONV - 1:GDN_CONV]
        for j in range(1, GDN_CONV):
            y = y + pltpu.roll(ext, j, 0)[SUBLANES:] * w[GDN_CONV - 1 - j:GDN_CONV - j]
        return y * _sigmoid(y)

    qc = conv(q_ref, pq_ref, iq_ref, cq_ref)
    kc = conv(k_ref, pk_ref, ik_ref, ck_ref)
    vc = conv(v_ref, pv_ref, iv_ref, cv_ref)

    misc = misc_ref[...]
    prm = prm_ref[...]
    beta_all = jnp.where(valid, _sigmoid(misc), 0.0)
    g_all = jnp.where(valid, -jnp.exp(prm[0:1]) * _softplus(misc + prm[1:2]), 0.0)
    gc_all = _cumsum_rows(causal.astype(BF16), g_all)
    gc_all_t = gc_all.T
    lane = lax.broadcasted_iota(jnp.int32, (L, LANES), 1)
    sub = lax.broadcasted_iota(jnp.int32, (LANES, L), 0)
    gz = gz_ref[...]
    nw = nw_ref[...]
    hd = GDN_HEAD

    heads = range(hb)
    sls = [slice(j * hd, (j + 1) * hd) for j in heads]
    ks, kq, decay, rhs, q_dec, k_end, dec_last = [], [], [], [], [], [], []
    for j in heads:
        hg = hg0 + j
        sl = sls[j]
        beta = jnp.sum(jnp.where(lane == lane_b + hg, beta_all, 0.0), axis=1, keepdims=True)
        gc = jnp.sum(jnp.where(lane == lane_a + hg, gc_all, 0.0), axis=1, keepdims=True)
        gr = jnp.sum(jnp.where(sub == lane_a + hg, gc_all_t, 0.0), axis=0, keepdims=True)
        gc_last = gc[L - 1:L]
        q = qc[:, sl]
        k = kc[:, sl]
        q = q * lax.rsqrt(jnp.sum(q * q, axis=-1, keepdims=True) + L2_EPS) * (hd ** -0.5)
        k = k * lax.rsqrt(jnp.sum(k * k, axis=-1, keepdims=True) + L2_EPS)
        e_gc = jnp.exp(gc)
        k_beta = k * beta
        ks.append(k)
        kq.append(jnp.concatenate([k_beta, q], axis=0))
        decay.append(jnp.where(causal, jnp.exp(jnp.where(causal, gc - gr, 0.0)), 0.0))
        rhs.append(jnp.concatenate([vc[:, sl] * beta, k_beta * e_gc], axis=1))
        q_dec.append(q * e_gc)
        k_end.append(k * jnp.exp(gc_last - gc))
        dec_last.append(jnp.exp(gc_last))
    sc = [_bdot_nt(kq[j], ks[j]) for j in heads]
    xs = [-jnp.where(strict, sc[j][:L] * decay[j], 0.0) for j in heads]
    sol = _unit_lower_solve(xs, rhs, n_stage)
    s_old = [s_ref[0, 0, j] for j in heads]
    ws = [_bdot(jnp.concatenate([sol[j][:, hd:], q_dec[j]], axis=0), s_old[j]) for j in heads]
    v_new = [sol[j][:, :hd] - ws[j][:L] for j in heads]
    os_ = [ws[j][L:] + _bdot(sc[j][L:] * decay[j], v_new[j]) for j in heads]
    for j in heads:
        s_ref[0, 0, j] = dec_last[j] * s_old[j] + _bdot_tn(k_end[j], v_new[j])
    for j in heads:
        o = os_[j]
        rms = o * lax.rsqrt(jnp.mean(o * o, axis=-1, keepdims=True) + RMS_EPS) * nw
        z = gz[:, sls[j]]
        o_ref[:, sls[j]] = rms * (z * _sigmoid(z))


def _gdn_call(u_main, u_misc, conv_init, conv_w, prm, norm_w, s0, lyr, acc, lyr_out, *, grp, D, lane_b, lane_a, hb):
    row0, B, Tp, L, t_real = grp
    H = D // GDN_HEAD
    hb = min(hb, H)
    W = hb * GDN_HEAD
    nc = Tp // L
    rb = lambda b, c: (row0 + b * Tp) // L + c
    pb = lambda b, c: jnp.maximum((row0 + b * Tp + c * L) // SUBLANES - 1, 0)
    off = 3 * D // W
    seg = D // W
    cur = lambda s: pl.BlockSpec((L, W), lambda b, h, c: (rb(b, c), off + s * seg + h))
    prev = lambda s: pl.BlockSpec((SUBLANES, W), lambda b, h, c: (pb(b, c), off + s * seg + h))
    init = lambda s: pl.BlockSpec((1, SUBLANES, W), lambda b, h, c: (b, 0, s * seg + h))
    cw = lambda s: pl.BlockSpec((GDN_CONV, W), lambda b, h, c: (0, s * seg + h))
    n_stage = int(math.log2(L))
    assert 1 << n_stage == L
    return pl.pallas_call(
        functools.partial(_gdn_kernel, L=L, t_real=t_real, hb=hb, lane_b=lane_b, lane_a=lane_a, n_stage=n_stage),
        grid=(B, H // hb, nc),
        in_specs=[cur(0), cur(1), cur(2), prev(0), prev(1), prev(2), init(0), init(1), init(2),
                  cw(0), cw(1), cw(2),
                  pl.BlockSpec((L, W), lambda b, h, c: (rb(b, c), 6 * D // W + h)),
                  pl.BlockSpec((L, LANES), lambda b, h, c: (rb(b, c), 0)),
                  pl.BlockSpec((SUBLANES, LANES), lambda b, h, c: (0, 0)),
                  pl.BlockSpec((1, GDN_HEAD), lambda b, h, c: (0, 0)),
                  pl.BlockSpec((1, 1, hb, GDN_HEAD, GDN_HEAD), lambda b, h, c: (lyr, b, h, 0, 0)),
                  pl.BlockSpec(memory_space=pl.ANY)],
        out_specs=[pl.BlockSpec((L, W), lambda b, h, c: (b * nc + c, h)),
                   pl.BlockSpec((1, 1, hb, GDN_HEAD, GDN_HEAD), lambda b, h, c: (lyr_out, b, h, 0, 0))],
        out_shape=[jax.ShapeDtypeStruct((B * Tp, D), F32),
                   jax.ShapeDtypeStruct(acc.shape, F32)],
        input_output_aliases={17: 1},
        compiler_params=_params(("parallel", "parallel", "arbitrary")),
        name="gdn_mixer",
    )(u_main, u_main, u_main, u_main, u_main, u_main, conv_init, conv_init, conv_init,
      conv_w, conv_w, conv_w, u_main, u_misc, prm, norm_w, s0, acc)


def _rwkv_kernel(r_ref, k_ref, v_ref, pr_ref, pk_ref, pv_ref, ir_ref, ik_ref, iv_ref,
                 misc_ref, pmisc_ref, imisc_ref, mur_ref, muk_ref, muv_ref, mum_ref,
                 w0_ref, w2_ref, a0_ref, a2_ref, g2_ref, kk_ref, ka_ref, rk_ref, lnw_ref, lnb_ref, s0_ref,
                 acc_ref, o_ref, s_ref, *, L, t_real, hb, n_stage):
    del acc_ref
    c = pl.program_id(2)

    @pl.when(c == 0)
    def _():
        s_ref[0, 0] = s0_ref[0, 0]

    row = lax.broadcasted_iota(jnp.int32, (L, 1), 0)
    valid = (c * L + row) < t_real
    causal, _, _ = _tri_masks(L)

    def mix(cur_ref, prev_ref, init_ref, mu_ref):
        cur = cur_ref[...]
        prev8 = jnp.where(c == 0, init_ref[0], prev_ref[...])
        ext = jnp.concatenate([prev8, cur], axis=0)
        prev = pltpu.roll(ext, 1, 0)[SUBLANES:]
        return cur + (prev - cur) * mu_ref[...]

    r = mix(r_ref, pr_ref, ir_ref, mur_ref)
    k = mix(k_ref, pk_ref, ik_ref, muk_ref)
    v = mix(v_ref, pv_ref, iv_ref, muv_ref)
    m = mix(misc_ref, pmisc_ref, imisc_ref, mum_ref)
    wl = m[:, MISC_WL:MISC_AL]
    al = m[:, MISC_AL:MISC_GL]
    gl = m[:, MISC_GL:]

    w_log = -_softplus(-(w0_ref[...] + _bdot(jnp.tanh(wl), w2_ref[...]))) - 0.5
    lw = jnp.where(valid, -jnp.exp(w_log), 0.0)
    a_sig = _sigmoid(a0_ref[...] + _bdot(al, a2_ref[...]))
    gate = _bdot(_sigmoid(gl), g2_ref[...])
    kk = k * kk_ref[...]
    k2 = jnp.where(valid, k * (1.0 + (a_sig - 1.0) * ka_ref[...]), 0.0)
    rk = rk_ref[...]
    lnw = lnw_ref[...]
    lnb = lnb_ref[...]
    cw = _cumsum_rows(causal.astype(BF16), lw)
    hd = RWKV_HEAD
    ri2 = lax.broadcasted_iota(jnp.int32, (2 * L, 2 * L), 0)
    ci2 = lax.broadcasted_iota(jnp.int32, (2 * L, 2 * L), 1)
    mask2 = ((ri2 & (L - 1)) - (ci2 & (L - 1)) + jnp.where(ri2 >= L, 1, 0)) > 0

    heads = range(hb)
    sls = [slice(j * hd, (j + 1) * hd) for j in heads]
    ar, bk, end, w_last = [], [], [], []
    for sl in sls:
        cw_h = cw[:, sl]
        cw_last = cw_h[L - 1:L]
        kk_h = kk[:, sl]
        kk_h = jnp.where(valid, kk_h * lax.rsqrt(jnp.sum(kk_h * kk_h, axis=-1, keepdims=True) + L2_EPS), 0.0)
        b_h = kk_h * a_sig[:, sl]
        e_neg = jnp.exp(-cw_h)
        e_end = jnp.exp(cw_last - cw_h)
        ar.append(jnp.concatenate([-kk_h * jnp.exp(cw_h - lw[:, sl]), r[:, sl] * jnp.exp(cw_h)], axis=0))
        bk.append(jnp.concatenate([b_h * e_neg, k2[:, sl] * e_neg], axis=0))
        end.append(jnp.concatenate([b_h * e_end, k2[:, sl] * e_end], axis=0))
        w_last.append(jnp.exp(cw_last))
    s_old = [s_ref[0, 0, j] for j in heads]
    g = [jnp.where(mask2, _bdot_nt(ar[j], bk[j]), 0.0) for j in heads]
    ars = [_bdot_nt(ar[j], s_old[j]) for j in heads]
    akv = [_bdot(g[j][:, L:], v[:, sls[j]]) for j in heads]
    u = _unit_lower_solve([g[j][:L, :L] for j in heads], [ars[j][:L] + akv[j][:L] for j in heads], n_stage)
    ys = [ars[j][L:] + akv[j][L:] + _bdot(g[j][L:, :L], u[j]) for j in heads]
    for j in heads:
        s_ref[0, 0, j] = s_old[j] * w_last[j] + _bdot_tn(jnp.concatenate([u[j], v[:, sls[j]]], axis=0), end[j])
    for j in heads:
        sl = sls[j]
        y = ys[j]
        mu = jnp.mean(y, axis=-1, keepdims=True)
        var = jnp.mean(jnp.square(y - mu), axis=-1, keepdims=True)
        yn = (y - mu) * lax.rsqrt(var + RWKV_GN_EPS) * lnw[:, sl] + lnb[:, sl]
        bonus = jnp.sum(r[:, sl] * k2[:, sl] * rk[:, sl], axis=-1, keepdims=True) * v[:, sl]
        o_ref[:, sl] = (yn + bonus) * gate[:, sl]


def _rwkv_call(u_main, u_misc, init_main, init_misc, pr, s0, lyr, acc, lyr_out, *, grp, D, hb):
    row0, B, Tp, L, t_real = grp
    H = D // RWKV_HEAD
    hb = min(hb, H)
    W = hb * RWKV_HEAD
    nc = Tp // L
    cm = u_misc.shape[1]
    rb = lambda b, c: (row0 + b * Tp) // L + c
    pb = lambda b, c: jnp.maximum((row0 + b * Tp + c * L) // SUBLANES - 1, 0)
    off = 7 * D // W
    seg = D // W
    cur = lambda s: pl.BlockSpec((L, W), lambda b, h, c: (rb(b, c), off + s * seg + h))
    prev = lambda s: pl.BlockSpec((SUBLANES, W), lambda b, h, c: (pb(b, c), off + s * seg + h))
    init = lambda s: pl.BlockSpec((1, SUBLANES, W), lambda b, h, c: (b, 0, s * seg + h))
    vec = lambda s: pl.BlockSpec((1, W), lambda b, h, c: (0, s * seg + h))
    full = lambda rows: pl.BlockSpec((rows, W), lambda b, h, c: (0, h))
    n_stage = int(math.log2(L))
    assert 1 << n_stage == L
    return pl.pallas_call(
        functools.partial(_rwkv_kernel, L=L, t_real=t_real, hb=hb, n_stage=n_stage),
        grid=(B, H // hb, nc),
        in_specs=[cur(0), cur(1), cur(2), prev(0), prev(1), prev(2), init(0), init(1), init(2),
                  pl.BlockSpec((L, cm), lambda b, h, c: (rb(b, c), 0)),
                  pl.BlockSpec((SUBLANES, cm), lambda b, h, c: (pb(b, c), 0)),
                  pl.BlockSpec((1, SUBLANES, cm), lambda b, h, c: (b, 0, 0)),
                  vec(0), vec(1), vec(2),
                  pl.BlockSpec((1, cm), lambda b, h, c: (0, 0)),
                  vec(0), full(LANES), vec(0), full(LANES), full(cm - MISC_GL),
                  vec(0), vec(0), vec(0), vec(0), vec(0),
                  pl.BlockSpec((1, 1, hb, RWKV_HEAD, RWKV_HEAD), lambda b, h, c: (lyr, b, h, 0, 0)),
                  pl.BlockSpec(memory_space=pl.ANY)],
        out_specs=[pl.BlockSpec((L, W), lambda b, h, c: (b * nc + c, h)),
                   pl.BlockSpec((1, 1, hb, RWKV_HEAD, RWKV_HEAD), lambda b, h, c: (lyr_out, b, h, 0, 0))],
        out_shape=[jax.ShapeDtypeStruct((B * Tp, D), F32),
                   jax.ShapeDtypeStruct(acc.shape, F32)],
        input_output_aliases={27: 1},
        compiler_params=_params(("parallel", "parallel", "arbitrary")),
        name="rwkv_mixer",
    )(u_main, u_main, u_main, u_main, u_main, u_main, init_main, init_main, init_main,
      u_misc, u_misc, init_misc, pr["mu_main"], pr["mu_main"], pr["mu_main"], pr["mu_misc"],
      pr["w0"], pr["w2"], pr["a0"], pr["a2"], pr["g2"], pr["k_k"], pr["k_a"], pr["r_k"],
      pr["ln_w"], pr["ln_b"], s0, acc)


def _merge_kernel(o0_ref, o1_ref, o2_ref, g0_ref, g1_ref, g2_ref, wb_ref, out_ref):
    acc = _sigmoid(g0_ref[...]) * _bdot(o0_ref[...], wb_ref[0])
    acc = acc + _sigmoid(g1_ref[...]) * _bdot(o1_ref[...], wb_ref[1])
    acc = acc + _sigmoid(g2_ref[...]) * _bdot(o2_ref[...], wb_ref[2])
    out_ref[...] = acc.astype(out_ref.dtype)


def _merge_call(o_gla, o_gdn, o_rwkv, u_main, w_branch, *, D, row0, tm, tn):
    n = o_gla.shape[0]
    gate_off = 10 * D // tn
    rb0 = row0 // tm
    o_spec = pl.BlockSpec((tm, D), lambda i, j: (i, 0))
    g_spec = lambda s: pl.BlockSpec((tm, tn), lambda i, j: (rb0 + i, gate_off + s * (D // tn) + j))
    return pl.pallas_call(
        _merge_kernel,
        grid=(n // tm, D // tn),
        in_specs=[o_spec, o_spec, o_spec, g_spec(0), g_spec(1), g_spec(2),
                  pl.BlockSpec((3, D, tn), lambda i, j: (0, 0, j))],
        out_specs=pl.BlockSpec((tm, tn), lambda i, j: (i, j)),
        out_shape=jax.ShapeDtypeStruct((n, D), BF16),
        compiler_params=_params(("parallel", "arbitrary")),
        name="branch_merge",
    )(o_gla, o_gdn, o_rwkv, u_main, u_main, u_main, w_branch)


def _layernorm(x, g, b):
    mu = jnp.mean(x, axis=-1, keepdims=True)
    var = jnp.mean(jnp.square(x - mu), axis=-1, keepdims=True)
    return (x - mu) * lax.rsqrt(var + LN_EPS) * g + b


def _outproj_kernel(m_ref, w_ref, x_ref, g_ref, b_ref, h_ref, *, alpha):
    y = alpha * x_ref[...] + jnp.dot(m_ref[...], w_ref[...], preferred_element_type=F32)
    h_ref[...] = _layernorm(y, g_ref[...], b_ref[...])


def _outproj_call(merged, w_out, x_pad, g, b, *, alpha, row0, tm):
    n, D = merged.shape
    rb0 = row0 // tm
    return pl.pallas_call(
        functools.partial(_outproj_kernel, alpha=alpha),
        grid=(n // tm,),
        in_specs=[pl.BlockSpec((tm, D), lambda i: (i, 0)),
                  pl.BlockSpec((D, D), lambda i: (0, 0)),
                  pl.BlockSpec((tm, D), lambda i: (rb0 + i, 0)),
                  pl.BlockSpec((1, D), lambda i: (0, 0)),
                  pl.BlockSpec((1, D), lambda i: (0, 0))],
        out_specs=pl.BlockSpec((tm, D), lambda i: (i, 0)),
        out_shape=jax.ShapeDtypeStruct((n, D), F32),
        compiler_params=_params(("parallel",)),
        name="out_proj_ln",
    )(merged, w_out, x_pad, g, b)


def _router_kernel(h_ref, wr_ref, bias_ref, ids_ref, wts_ref, *, n_exp):
    tm = h_ref.shape[0]
    logits = _bdot_nt(wr_ref[...], h_ref[...])[:n_exp]
    scores = _sigmoid(logits)
    sel = scores + bias_ref[...][:n_exp]
    gsz = n_exp // N_GROUPS
    neg = -jnp.inf

    def first_argmax(vals, iota, n):
        mx = jnp.max(vals, axis=0, keepdims=True)
        idx = jnp.min(jnp.where(vals == mx, iota, n), axis=0, keepdims=True)
        return mx, idx

    io_g = lax.broadcasted_iota(jnp.int32, (gsz, tm), 0)
    grp_rows = []
    for gi in range(N_GROUPS):
        blk = sel[gi * gsz:(gi + 1) * gsz]
        m1, i1 = first_argmax(blk, io_g, gsz)
        m2 = jnp.max(jnp.where(io_g == i1, neg, blk), axis=0, keepdims=True)
        grp_rows.append(m1 + m2)
    grp = jnp.concatenate(grp_rows, axis=0)
    io_n = lax.broadcasted_iota(jnp.int32, (N_GROUPS, tm), 0)
    gmask = jnp.zeros((N_GROUPS, tm), F32)
    for _ in range(TOPK_GROUPS):
        _, gi = first_argmax(grp, io_n, N_GROUPS)
        hit = io_n == gi
        gmask = jnp.where(hit, 1.0, gmask)
        grp = jnp.where(hit, neg, grp)
    io_e = lax.broadcasted_iota(jnp.int32, (n_exp, tm), 0)
    emask = jnp.concatenate(
        [jnp.broadcast_to(gmask[gi:gi + 1], (gsz, tm)) for gi in range(N_GROUPS)], axis=0)
    cand = jnp.where(emask > 0.5, sel, neg)
    ids, wts = [], []
    for _ in range(TOP_K):
        _, ei = first_argmax(cand, io_e, n_exp)
        hit = io_e == ei
        ids.append(ei)
        wts.append(jnp.sum(jnp.where(hit, scores, 0.0), axis=0, keepdims=True))
        cand = jnp.where(hit, neg, cand)
    w = jnp.concatenate(wts, axis=0)
    ids_ref[...] = jnp.concatenate(ids, axis=0)
    wts_ref[...] = w / (jnp.sum(w, axis=0, keepdims=True) + 1e-20) * ROUTED_SCALE


def _router_call(h, wr_t, bias_col, *, n_exp, tm):
    n, D = h.shape
    return pl.pallas_call(
        functools.partial(_router_kernel, n_exp=n_exp),
        grid=(n // tm,),
        in_specs=[pl.BlockSpec((tm, D), lambda i: (i, 0)),
                  pl.BlockSpec(wr_t.shape, lambda i: (0, 0)),
                  pl.BlockSpec(bias_col.shape, lambda i: (0, 0))],
        out_specs=[pl.BlockSpec((TOP_K, tm), lambda i: (0, i)),
                   pl.BlockSpec((TOP_K, tm), lambda i: (0, i))],
        out_shape=[jax.ShapeDtypeStruct((TOP_K, n), jnp.int32),
                   jax.ShapeDtypeStruct((TOP_K, n), F32)],
        compiler_params=_params(("parallel",)),
        name="router_topk",
    )(h, wr_t, bias_col)


DMA_UNROLL = 8
GDN_HEADS_PER_STEP = (16, 16)
RWKV_HEADS_PER_STEP = (32, 16)


def _swiglu(x_bf, w_gu, w_down):
    f = w_down.shape[0]
    gu = jnp.dot(x_bf, w_gu, preferred_element_type=F32)
    gt = gu[:, :f]
    act = gt * _sigmoid(gt) * gu[:, f:]
    return jnp.dot(act.astype(BF16), w_down, preferred_element_type=F32)


def _moe_kernel(blk_e_ref, n_used_ref, src_ref, nsrc_ref, h_hbm, sw_ref, wgu_ref, wd_ref, y_ref,
                xbuf, sem_in, *, tm):
    i = pl.program_id(0)
    n_used = n_used_ref[0]
    slot = i % 2

    def gather(idx_ref, buf_slot, unroll):
        def body(r, carry):
            pltpu.make_async_copy(h_hbm.at[pl.ds(idx_ref[0, 0, r], 1)], xbuf.at[buf_slot, pl.ds(r, 1)],
                                  sem_in.at[buf_slot]).start()
            return carry
        lax.fori_loop(0, tm, body, 0, unroll=unroll)

    def wait(buf_slot):
        pltpu.make_async_copy(xbuf.at[buf_slot], xbuf.at[buf_slot], sem_in.at[buf_slot]).wait()

    @pl.when(i == 0)
    def _():
        gather(src_ref, 0, DMA_UNROLL)

    @pl.when(i < n_used)
    def _():
        wait(slot)
        gather(nsrc_ref, 1 - slot, True)
        y_ref[...] = _swiglu(xbuf[slot].astype(BF16), wgu_ref[0, 0], wd_ref[0, 0]) * sw_ref[...]

    @pl.when(i == n_used)
    def _():
        wait(slot)

    @pl.when(i >= n_used)
    def _():
        y_ref[...] = jnp.zeros_like(y_ref)


def _moe_call(blk_e, n_used, src, h, slot_w, w_gu, w_down, lyr, *, tm):
    n_blocks = src.shape[0]
    n, D = h.shape
    f2 = w_gu.shape[3]
    f = w_down.shape[2]
    grid_spec = pltpu.PrefetchScalarGridSpec(
        num_scalar_prefetch=2,
        grid=(n_blocks,),
        in_specs=[pl.BlockSpec((1, 1, tm), lambda i, be, nu: (i, 0, 0), memory_space=pltpu.SMEM),
                  pl.BlockSpec((1, 1, tm), lambda i, be, nu: (jnp.minimum(i + 1, n_blocks - 1), 0, 0),
                               memory_space=pltpu.SMEM),
                  pl.BlockSpec(memory_space=pl.ANY),
                  pl.BlockSpec((tm, 1), lambda i, be, nu: (i, 0)),
                  pl.BlockSpec((1, 1, D, f2), lambda i, be, nu: (lyr, be[i], 0, 0)),
                  pl.BlockSpec((1, 1, f, D), lambda i, be, nu: (lyr, be[i], 0, 0))],
        out_specs=pl.BlockSpec((tm, D), lambda i, be, nu: (i, 0)),
        scratch_shapes=[pltpu.VMEM((2, tm, D), F32), pltpu.SemaphoreType.DMA((2,))],
    )
    return pl.pallas_call(
        functools.partial(_moe_kernel, tm=tm),
        grid_spec=grid_spec,
        out_shape=jax.ShapeDtypeStruct((n_blocks * tm, D), F32),
        compiler_params=_params(("arbitrary",)),
        name="moe_experts",
    )(blk_e, n_used, src, src, h, slot_w, w_gu, w_down)


def _combine_kernel(dest_ref, y_hbm, h_ref, wgu_ref, wd_ref, g_ref, b_ref, o_ref, gbuf, sem, *, alpha, nk, tm):
    for k in range(nk):
        def body(r, carry, k=k):
            pltpu.make_async_copy(y_hbm.at[pl.ds(dest_ref[0, 0, k * tm + r], 1)], gbuf.at[k, pl.ds(r, 1)],
                                  sem).start()
            return carry
        lax.fori_loop(0, tm, body, 0, unroll=DMA_UNROLL)

    h = h_ref[...]
    acc = alpha * h + _swiglu(h.astype(BF16), wgu_ref[...], wd_ref[...])
    pltpu.make_async_copy(gbuf, gbuf, sem).wait()
    for k in range(nk):
        acc = acc + gbuf[k]
    o_ref[...] = _layernorm(acc, g_ref[...], b_ref[...])


def _combine_call(dest, y_sorted, h, w_sh_gu, w_sh_down, g, b, *, alpha, tm, nk):
    n, D = h.shape
    f2 = w_sh_gu.shape[1]
    f = w_sh_down.shape[0]
    return pl.pallas_call(
        functools.partial(_combine_kernel, alpha=alpha, nk=nk, tm=tm),
        grid=(n // tm,),
        in_specs=[pl.BlockSpec((1, 1, nk * tm), lambda i: (i, 0, 0), memory_space=pltpu.SMEM),
                  pl.BlockSpec(memory_space=pl.ANY),
                  pl.BlockSpec((tm, D), lambda i: (i, 0)),
                  pl.BlockSpec((D, f2), lambda i: (0, 0)),
                  pl.BlockSpec((f, D), lambda i: (0, 0)),
                  pl.BlockSpec((1, D), lambda i: (0, 0)),
                  pl.BlockSpec((1, D), lambda i: (0, 0))],
        out_specs=pl.BlockSpec((tm, D), lambda i: (i, 0)),
        out_shape=jax.ShapeDtypeStruct((n, D), F32),
        scratch_shapes=[pltpu.VMEM((nk, tm, D), F32), pltpu.SemaphoreType.DMA(())],
        compiler_params=_params(("arbitrary",)),
        name="moe_combine_ln",
    )(dest, y_sorted, h, w_sh_gu, w_sh_down, g, b)


def _dispatch_tables(ids, wts, *, n_exp_tot, tm, n_blocks):
    n, nk = ids.shape
    onehot = (ids[:, :, None] == jnp.arange(n_exp_tot, dtype=jnp.int32)).any(axis=1)
    mask = onehot.astype(jnp.int32)
    cum = jnp.cumsum(mask, axis=0) - mask
    pos = jnp.take_along_axis(cum, ids, axis=1)
    counts = mask.sum(axis=0)
    padded = (counts + tm - 1) // tm * tm
    pad_end = jnp.cumsum(padded)
    pad_start = pad_end - padded
    dest = pad_start[ids] + pos
    n_used = (pad_end[-1] // tm).astype(jnp.int32).reshape(1)
    p = n_blocks * tm
    slot_a = jnp.full((p,), -1, jnp.int32).at[dest.reshape(-1)].set(jnp.arange(n * nk, dtype=jnp.int32))
    used = slot_a >= 0
    a = jnp.maximum(slot_a, 0)
    src = jnp.where(used, a // nk, 0)
    slot_w = jnp.where(used, wts.reshape(-1)[a], 0.0)
    blk_start = jnp.arange(n_blocks, dtype=jnp.int32) * tm
    blk_e = jnp.minimum(jnp.sum((pad_end[None, :] <= blk_start[:, None]).astype(jnp.int32), axis=1), n_exp_tot - 1)
    dest_t = dest.reshape(n // tm, tm, nk).transpose(0, 2, 1).reshape(n // tm, 1, nk * tm)
    return blk_e, n_used, src.reshape(n_blocks, 1, tm), slot_w.reshape(p, 1), dest_t


def _pad_rows(a, lanes):
    return jnp.pad(a, ((0, 0), (0, lanes - a.shape[1])))


def _layer(x, lyr, st, w, *, dims):
    D, Bp, T, Bs, Ts, Tsp = dims
    n_p = Bp * T
    n_s = Bs * Ts
    n = n_p + n_s
    depth = w["w_in"].shape[0]
    alpha = (2 * depth) ** 0.25
    Lp = math.gcd(T, CHUNK)
    grp_p = (0, Bp, T, Lp, T)
    grp_s = (n_p, Bs, Tsp, Tsp, Ts)
    H_gdn = D // GDN_HEAD
    lora = w["gla_w_lora"].shape[1]
    lane_b, lane_a = lora, lora + H_gdn
    wl_n, al_n, gl_n = w["rwkv_w2"].shape[1], w["rwkv_a2"].shape[1], w["rwkv_g2"].shape[1]
    assert lane_a + H_gdn <= LANES and wl_n <= LANES and al_n <= LANES and gl_n % LANES == 0
    assert T % Lp == 0 and n_p % Tsp == 0 and Ts >= GDN_CONV - 1 and Tsp % SUBLANES == 0

    wi = w["w_in"][lyr]
    o_glr = 3 * D
    o_gdn = o_glr + lora
    o_gb = o_gdn + 4 * D
    o_ga = o_gb + H_gdn
    o_rw = o_ga + H_gdn
    o_wl = o_rw + 3 * D
    o_al = o_wl + wl_n
    o_gl = o_al + al_n
    o_gate = o_gl + gl_n
    w_main = jnp.concatenate([wi[:, :3 * D], wi[:, o_gdn:o_gdn + 4 * D], wi[:, o_rw:o_rw + 3 * D],
                              wi[:, o_gate:o_gate + 3 * D]], axis=1).astype(BF16)
    w_misc = jnp.concatenate([
        _pad_rows(jnp.concatenate([wi[:, o_glr:o_glr + lora], wi[:, o_gb:o_gb + 2 * H_gdn]], axis=1), LANES),
        _pad_rows(wi[:, o_wl:o_wl + wl_n], LANES), _pad_rows(wi[:, o_al:o_al + al_n], LANES),
        wi[:, o_gl:o_gl + gl_n]], axis=1).astype(BF16)
    cm = w_misc.shape[1]

    xs_pad = jnp.pad(x[n_p:].reshape(Bs, Ts, D), ((0, 0), (0, Tsp - Ts), (0, 0))).reshape(Bs * Tsp, D)
    x_pad = jnp.concatenate([x[:n_p], xs_pad], axis=0)
    np_rows = x_pad.shape[0]
    x_bf = x_pad.astype(BF16)

    tm = _pick(np_rows, (1024, 512, 256, 128, 96, 64, 32, 16, 8))
    u_main = _matmul(x_bf, w_main, tm, _pick(w_main.shape[1], (1024, 512, 256, 128)))
    u_misc = _matmul(x_bf, w_misc, tm, cm)

    wl_ext = jnp.pad(w["gla_w_lora"][lyr], ((0, LANES - lora), (0, 0))).astype(BF16)
    b_lora = w["gla_b_lora"][lyr][None]
    gla_nw = w["gla_norm_w"][lyr][None]
    conv_w = w["gdn_conv_w"][lyr]
    gdn_prm = jnp.zeros((SUBLANES, LANES), F32)
    gdn_prm = gdn_prm.at[0, lane_a:lane_a + H_gdn].set(w["gdn_A_log"][lyr])
    gdn_prm = gdn_prm.at[1, lane_a:lane_a + H_gdn].set(w["gdn_dt_bias"][lyr])
    gdn_nw = w["gdn_norm_w"][lyr][None]
    mu = w["rwkv_mu"][lyr]

    def misc_layout(a):
        z = jnp.zeros(a.shape[:-1] + (LANES,), a.dtype)
        return jnp.concatenate([z, _pad_last(a[..., :wl_n]), _pad_last(a[..., wl_n:wl_n + al_n]),
                                a[..., wl_n + al_n:]], axis=-1)

    def _pad_last(a):
        return jnp.pad(a, [(0, 0)] * (a.ndim - 1) + [(0, LANES - a.shape[-1])])

    rw = {
        "mu_main": mu[None, :3 * D], "mu_misc": misc_layout(mu[None, 3 * D:]),
        "w0": w["rwkv_w0"][lyr][None],
        "w2": jnp.pad(w["rwkv_w2"][lyr], ((0, LANES - wl_n), (0, 0))).astype(BF16),
        "a0": w["rwkv_a0"][lyr][None],
        "a2": jnp.pad(w["rwkv_a2"][lyr], ((0, LANES - al_n), (0, 0))).astype(BF16),
        "g2": w["rwkv_g2"][lyr].astype(BF16),
        "k_k": w["rwkv_k_k"][lyr][None], "k_a": w["rwkv_k_a"][lyr][None],
        "r_k": w["rwkv_r_k"][lyr].reshape(1, D),
        "ln_w": w["rwkv_ln_w"][lyr][None], "ln_b": w["rwkv_ln_b"][lyr][None],
    }

    w_branch = w["w_branch"][lyr].astype(BF16)
    w_out = w["w_out"][lyr].astype(BF16)
    ln1_g, ln1_b = w["ln1_g"][lyr][None], w["ln1_b"][lyr][None]
    h_groups = []
    new_states = []
    for grp, s in ((grp_p, st["p"]), (grp_s, st["s"])):
        row0, B, Tp, L, t_real = grp
        l_idx = lyr if s["layered"] else 0
        conv_init = jnp.pad(s["conv"], ((0, 0), (SUBLANES - (GDN_CONV - 1), 0), (0, 0)))
        shift = s["shift"]
        init_main = jnp.pad(shift[:, None, :3 * D], ((0, 0), (SUBLANES - 1, 0), (0, 0)))
        init_misc = jnp.pad(misc_layout(shift[:, None, 3 * D:]), ((0, 0), (SUBLANES - 1, 0), (0, 0)))
        acc = s["acc"]
        o_gla, s_gla = _gla_call(u_main, u_misc, wl_ext, b_lora, gla_nw, s["gla"], l_idx, acc[0], lyr, grp=grp, D=D,
                                 hb=GLA_HEADS)
        o_gdn, s_gdn = _gdn_call(u_main, u_misc, conv_init, conv_w, gdn_prm, gdn_nw, s["gdn"], l_idx, acc[1], lyr,
                                 grp=grp, D=D, lane_b=lane_b, lane_a=lane_a, hb=GDN_HEADS_PER_STEP[L >= CHUNK])
        o_rwkv, s_rwkv = _rwkv_call(u_main, u_misc, init_main, init_misc, rw, s["rwkv"], l_idx, acc[2], lyr,
                                    grp=grp, D=D, hb=RWKV_HEADS_PER_STEP[L >= CHUNK])
        tm2 = _pick(math.gcd(B * Tp, row0) if row0 else B * Tp, (512, 256, 128, 64, 32, 16))
        merged = _merge_call(o_gla, o_gdn, o_rwkv, u_main, w_branch, D=D, row0=row0, tm=tm2,
                             tn=_pick(D, (512, 256, 128)))
        h_groups.append(_outproj_call(merged, w_out, x_pad, ln1_g, ln1_b, alpha=alpha, row0=row0, tm=tm2))
        nc_rows = GDN_CONV - 1
        last_rows = (row0 + np.arange(B)[:, None] * Tp + np.arange(t_real - nc_rows, t_real)[None, :]).reshape(-1)
        u_last = jnp.take(u_main, jnp.asarray(last_rows, jnp.int32), axis=0).reshape(B, nc_rows, -1)
        m_last = jnp.take(u_misc, jnp.asarray(last_rows[nc_rows - 1::nc_rows], jnp.int32), axis=0)
        new_conv = u_last[:, :, 3 * D:6 * D]
        new_shift = jnp.concatenate([u_last[:, -1, 7 * D:10 * D], m_last[:, MISC_WL:MISC_WL + wl_n],
                                     m_last[:, MISC_AL:MISC_AL + al_n], m_last[:, MISC_GL:]], axis=-1)
        new_states.append((s_gla, s_gdn, new_conv, s_rwkv, new_shift))

    h = jnp.concatenate([h_groups[0], h_groups[1].reshape(Bs, Tsp, D)[:, :Ts].reshape(n_s, D)], axis=0)

    n_exp = w["w_router"].shape[2]
    wr_t = jnp.pad(w["w_router"][lyr].T, ((0, LANES - n_exp), (0, 0))).astype(BF16)
    bias_col = jnp.pad(w["router_bias"][lyr], (0, LANES - n_exp))[:, None]
    tmr = _pick(n, (512, 256, 128, 64, 32, 16, 8))
    ids_t, wts_t = _router_call(h, wr_t, bias_col, n_exp=n_exp, tm=tmr)
    ids, wts = ids_t.T, wts_t.T
    tme = _pick(n, (256, 128, 64, 32, 16, 8))
    n_blocks = -(-n * TOP_K // tme) + n_exp + 1
    blk_e, n_used, src, slot_w, dest_t = _dispatch_tables(ids, wts, n_exp_tot=n_exp, tm=tme, n_blocks=n_blocks)
    y_sorted = _moe_call(blk_e, n_used, src, h, slot_w, w["w_exp_gu_bf"], w["w_exp_down_bf"], lyr, tm=tme)
    x_out = _combine_call(dest_t, y_sorted, h, w["w_sh_gu"][lyr].astype(BF16), w["w_sh_down"][lyr].astype(BF16),
                          w["ln2_g"][lyr][None], w["ln2_b"][lyr][None], alpha=alpha, tm=tme, nk=TOP_K)
    return x_out, new_states


def kernel(x_prompt, x_sample, state_gla, state_gdn, state_gdn_conv, state_rwkv, state_rwkv_shift, w_in, gla_w_lora, gla_b_lora, gla_norm_w, gdn_conv_w, gdn_A_log, gdn_dt_bias, gdn_norm_w, rwkv_mu, rwkv_w0, rwkv_w2, rwkv_a0, rwkv_a2, rwkv_g2, rwkv_k_k, rwkv_k_a, rwkv_r_k, rwkv_ln_w, rwkv_ln_b, w_branch, w_out, ln1_g, ln1_b, w_router, router_bias, w_exp_gu, w_exp_down, w_sh_gu, w_sh_down, ln2_g, ln2_b):
    w = dict(w_in=w_in, gla_w_lora=gla_w_lora, gla_b_lora=gla_b_lora, gla_norm_w=gla_norm_w,
             gdn_conv_w=gdn_conv_w, gdn_A_log=gdn_A_log, gdn_dt_bias=gdn_dt_bias, gdn_norm_w=gdn_norm_w,
             rwkv_mu=rwkv_mu, rwkv_w0=rwkv_w0, rwkv_w2=rwkv_w2, rwkv_a0=rwkv_a0, rwkv_a2=rwkv_a2,
             rwkv_g2=rwkv_g2, rwkv_k_k=rwkv_k_k, rwkv_k_a=rwkv_k_a, rwkv_r_k=rwkv_r_k, rwkv_ln_w=rwkv_ln_w,
             rwkv_ln_b=rwkv_ln_b, w_branch=w_branch, w_out=w_out, ln1_g=ln1_g, ln1_b=ln1_b,
             w_router=w_router, router_bias=router_bias, w_exp_gu=w_exp_gu, w_exp_down=w_exp_down,
             w_sh_gu=w_sh_gu, w_sh_down=w_sh_down, ln2_g=ln2_g, ln2_b=ln2_b)
    Bp, T, D = x_prompt.shape
    Bs, Ts, _ = x_sample.shape
    depth = w_in.shape[0]
    Tsp = -(-Ts // SUBLANES) * SUBLANES
    dims = (D, Bp, T, Bs, Ts, Tsp)
    conv_cols = state_gdn_conv.shape[-1]
    shift_cols = state_rwkv_shift.shape[-1]
    zero_p = dict(layered=False,
                  gla=jnp.zeros((1, Bp) + state_gla.shape[2:], F32),
                  gdn=jnp.zeros((1, Bp) + state_gdn.shape[2:], F32),
                  rwkv=jnp.zeros((1, Bp) + state_rwkv.shape[2:], F32),
                  conv=jnp.zeros((Bp, GDN_CONV - 1, conv_cols), F32),
                  shift=jnp.zeros((Bp, shift_cols), F32))
    w["w_exp_gu_bf"] = w_exp_gu.astype(BF16)
    w["w_exp_down_bf"] = w_exp_down.astype(BF16)
    x = jnp.concatenate([x_prompt.reshape(Bp * T, D), x_sample.reshape(Bs * Ts, D)], axis=0)
    acc = {g: [jnp.zeros((depth, B) + s.shape[2:], F32) for s in (state_gla, state_gdn, state_rwkv)]
           for g, B in (("p", Bp), ("s", Bs))}
    small = {"p": [], "s": []}
    for lyr in range(depth):
        st = {"p": dict(zero_p, acc=acc["p"]),
              "s": dict(layered=True, gla=state_gla, gdn=state_gdn, rwkv=state_rwkv,
                        conv=state_gdn_conv[lyr], shift=state_rwkv_shift[lyr], acc=acc["s"])}
        x, new = _layer(x, lyr, st, w, dims=dims)
        for g, (s_gla, s_gdn, new_conv, s_rwkv, new_shift) in zip(("p", "s"), new):
            acc[g] = [s_gla, s_gdn, s_rwkv]
            small[g].append((new_conv, new_shift))

    def group_out(g):
        conv = jnp.stack([c for c, _ in small[g]])
        shift = jnp.stack([s for _, s in small[g]])
        return (acc[g][0], acc[g][1], conv, acc[g][2], shift)

    n_p = Bp * T
    return (x[:n_p].reshape(Bp, T, D), x[n_p:].reshape(Bs, Ts, D)) + group_out("p") + group_out("s")
```

```python
import functools
import math

import jax
import jax.numpy as jnp
import numpy as np
from jax import lax
from jax.experimental import pallas as pl
from jax.experimental.pallas import tpu as pltpu

F32 = jnp.float32
BF16 = jnp.bfloat16

GLA_HEADS = 4
GLA_GATE_TAU = 16.0
GDN_HEAD = 128
GDN_CONV = 4
RWKV_HEAD = 64
RWKV_GN_EPS = 64e-5
CHUNK = 64
TOP_K = 8
N_GROUPS = 8
TOPK_GROUPS = 4
ROUTED_SCALE = 2.5
LN_EPS = 1e-5
RMS_EPS = 1e-6
L2_EPS = 1e-6

LANES = 128
SUBLANES = 8
VMEM_LIMIT = 56 * 1024 * 1024

MISC_WL = 128
MISC_AL = 256
MISC_GL = 384


def _pick(n, cands):
    for c in cands:
        if n % c == 0:
            return c
    raise ValueError(f"no tile for {n}")


def _params(sem):
    return pltpu.CompilerParams(dimension_semantics=sem, vmem_limit_bytes=VMEM_LIMIT)


def _bdot(a, b):
    return jnp.dot(a.astype(BF16), b.astype(BF16), preferred_element_type=F32)


def _bdot_nt(a, b):
    return lax.dot_general(a.astype(BF16), b.astype(BF16), (((1,), (1,)), ((), ())), preferred_element_type=F32)


def _bdot_tn(a, b):
    return lax.dot_general(a.astype(BF16), b.astype(BF16), (((0,), (0,)), ((), ())), preferred_element_type=F32)


def _softplus(x):
    return jnp.maximum(x, 0.0) + jnp.log1p(jnp.exp(-jnp.abs(x)))


def _sigmoid(x):
    return 1.0 / (1.0 + jnp.exp(-x))


def _tri_masks(L):
    ri = lax.broadcasted_iota(jnp.int32, (L, L), 0)
    ci = lax.broadcasted_iota(jnp.int32, (L, L), 1)
    return ri >= ci, ri > ci, ri == ci


def _split2(x):
    hi = x.astype(BF16)
    lo = (x - hi.astype(F32)).astype(BF16)
    return hi, lo


def _cumsum_rows(tril_bf, x):
    hi = x.astype(BF16)
    r1 = x - hi.astype(F32)
    mid = r1.astype(BF16)
    lo = (r1 - mid.astype(F32)).astype(BF16)
    d = lambda y: jnp.dot(tril_bf, y, preferred_element_type=F32)
    return d(hi) + (d(mid) + d(lo))


def _unit_lower_solve(xs, ws, n_stage):
    n = ws[0].shape[1]
    ts = list(xs)
    ws = list(ws)
    d = lambda x, y: jnp.dot(x, y, preferred_element_type=F32)
    for s in range(n_stage):
        last = s == n_stage - 1
        outs = []
        for t, w in zip(ts, ws):
            th, tl = _split2(t)
            wh, wl = _split2(w)
            rh = wh if last else jnp.concatenate([wh, th], axis=1)
            rl = wl if last else jnp.concatenate([wl, tl], axis=1)
            outs.append(d(th, rh) + (d(th, rl) + d(tl, rh)))
        ws = [w + o[:, :n] for w, o in zip(ws, outs)]
        if not last:
            ts = [o[:, n:] for o in outs]
    return ws


def _mm_kernel(x_ref, w_ref, o_ref):
    o_ref[...] = jnp.dot(x_ref[...], w_ref[...], preferred_element_type=F32)


def _matmul(x, w, tm, tn):
    n, k = x.shape
    c = w.shape[1]
    return pl.pallas_call(
        _mm_kernel,
        grid=(n // tm, c // tn),
        in_specs=[pl.BlockSpec((tm, k), lambda i, j: (i, 0)),
                  pl.BlockSpec((k, tn), lambda i, j: (0, j))],
        out_specs=pl.BlockSpec((tm, tn), lambda i, j: (i, j)),
        out_shape=jax.ShapeDtypeStruct((n, c), F32),
        compiler_params=_params(("parallel", "parallel")),
        name="in_proj",
    )(x, w)


def _gla_kernel(q_ref, k_ref, v_ref, og_ref, misc_ref, wl_ref, bl_ref, nw_ref, s0_ref, acc_ref, o_ref, s_ref,
                *, L, t_real, hb):
    del acc_ref
    c = pl.program_id(2)

    @pl.when(c == 0)
    def _():
        s_ref[0, 0] = s0_ref[0, 0]

    dk = q_ref.shape[1] // hb
    dv = v_ref.shape[1] // hb
    row = lax.broadcasted_iota(jnp.int32, (L, 1), 0)
    valid = (c * L + row) < t_real
    causal, _, _ = _tri_masks(L)

    z = _bdot(misc_ref[...], wl_ref[...]) + bl_ref[...]
    log_a = jnp.where(valid, -_softplus(-z) / GLA_GATE_TAU, 0.0)
    q = q_ref[...]
    k = jnp.where(valid, k_ref[...], 0.0)
    v = jnp.where(valid, v_ref[...], 0.0)

    g = _cumsum_rows(causal.astype(BF16), log_a)
    g_last = g[L - 1:L, :]
    qg = q * (dk ** -0.5) * jnp.exp(g)
    kg = k * jnp.exp(-g)
    k_end = k * jnp.exp(g_last - g)
    dec = jnp.exp(g_last)
    ri = lax.broadcasted_iota(jnp.int32, (dk, dk), 0)
    ci = lax.broadcasted_iota(jnp.int32, (dk, dk), 1)
    nw = nw_ref[...]
    og = og_ref[...]

    heads = range(hb)
    ks = [slice(j * dk, (j + 1) * dk) for j in heads]
    vs = [slice(j * dv, (j + 1) * dv) for j in heads]
    a = [jnp.where(causal, _bdot_nt(qg[:, ks[j]], kg[:, ks[j]]), 0.0) for j in heads]
    s_old = [s_ref[0, 0, j] for j in heads]
    os_ = [_bdot(a[j], v[:, vs[j]]) + _bdot(qg[:, ks[j]], s_old[j]) for j in heads]
    for j in heads:
        dec_col = jnp.sum(jnp.where(ri == ci, jnp.broadcast_to(dec[:, ks[j]], (dk, dk)), 0.0), axis=1, keepdims=True)
        s_ref[0, 0, j] = dec_col * s_old[j] + _bdot_tn(k_end[:, ks[j]], v[:, vs[j]])
    for j in heads:
        o = os_[j]
        rms = o * lax.rsqrt(jnp.mean(o * o, axis=-1, keepdims=True) + RMS_EPS) * nw
        z_g = og[:, vs[j]]
        o_ref[:, vs[j]] = rms * (z_g * _sigmoid(z_g))


def _gla_call(u_main, u_misc, wl_ext, b_lora, norm_w, s0, lyr, acc, lyr_out, *, grp, D, hb):
    row0, B, Tp, L, t_real = grp
    H = GLA_HEADS
    dv = D // H
    dk = dv // 2
    nc = Tp // L
    wk, wv = hb * dk, hb * dv
    rb = lambda b, c: (row0 + b * Tp) // L + c
    return pl.pallas_call(
        functools.partial(_gla_kernel, L=L, t_real=t_real, hb=hb),
        grid=(B, H // hb, nc),
        in_specs=[
            pl.BlockSpec((L, wk), lambda b, h, c: (rb(b, c), h)),
            pl.BlockSpec((L, wk), lambda b, h, c: (rb(b, c), (D // 2) // wk + h)),
            pl.BlockSpec((L, wv), lambda b, h, c: (rb(b, c), D // wv + h)),
            pl.BlockSpec((L, wv), lambda b, h, c: (rb(b, c), 2 * D // wv + h)),
            pl.BlockSpec((L, LANES), lambda b, h, c: (rb(b, c), 0)),
            pl.BlockSpec((LANES, wk), lambda b, h, c: (0, h)),
            pl.BlockSpec((1, wk), lambda b, h, c: (0, h)),
            pl.BlockSpec((1, dv), lambda b, h, c: (0, 0)),
            pl.BlockSpec((1, 1, hb, dk, dv), lambda b, h, c: (lyr, b, h, 0, 0)),
            pl.BlockSpec(memory_space=pl.ANY),
        ],
        out_specs=[
            pl.BlockSpec((L, wv), lambda b, h, c: (b * nc + c, h)),
            pl.BlockSpec((1, 1, hb, dk, dv), lambda b, h, c: (lyr_out, b, h, 0, 0)),
        ],
        out_shape=[jax.ShapeDtypeStruct((B * Tp, D), F32),
                   jax.ShapeDtypeStruct(acc.shape, F32)],
        input_output_aliases={9: 1},
        compiler_params=_params(("parallel", "parallel", "arbitrary")),
        name="gla_mixer",
    )(u_main, u_main, u_main, u_main, u_misc, wl_ext, b_lora, norm_w, s0, acc)


def _gdn_kernel(q_ref, k_ref, v_ref, pq_ref, pk_ref, pv_ref, iq_ref, ik_ref, iv_ref,
                cq_ref, ck_ref, cv_ref, gz_ref, misc_ref, prm_ref, nw_ref, s0_ref, acc_ref, o_ref, s_ref,
                *, L, t_real, hb, lane_b, lane_a, n_stage):
    del acc_ref
    c = pl.program_id(2)
    hg0 = pl.program_id(1) * hb

    @pl.when(c == 0)
    def _():
        s_ref[0, 0] = s0_ref[0, 0]

    row = lax.broadcasted_iota(jnp.int32, (L, 1), 0)
    valid = (c * L + row) < t_real
    causal, strict, _ = _tri_masks(L)

    def conv(cur_ref, prev_ref, init_ref, w_ref):
        cur = cur_ref[...]
        prev = jnp.where(c == 0, init_ref[0], prev_ref[...])
        ext = jnp.concatenate([prev, cur], axis=0)
        w = w_ref[...]
        y = cur * w[GDN_CONV - 1:GDN_CONV]
        for j in range(1, GDN_CONV):
            y = y + pltpu.roll(ext, j, 0)[SUBLANES:] * w[GDN_CONV - 1 - j:GDN_CONV - j]
        return y * _sigmoid(y)

    qc = conv(q_ref, pq_ref, iq_ref, cq_ref)
    kc = conv(k_ref, pk_ref, ik_ref, ck_ref)
    vc = conv(v_ref, pv_ref, iv_ref, cv_ref)

    misc = misc_ref[...]
    prm = prm_ref[...]
    beta_all = jnp.where(valid, _sigmoid(misc), 0.0)
    g_all = jnp.where(valid, -jnp.exp(prm[0:1]) * _softplus(misc + prm[1:2]), 0.0)
    gc_all = _cumsum_rows(causal.astype(BF16), g_all)
    gc_all_t = gc_all.T
    lane = lax.broadcasted_iota(jnp.int32, (L, LANES), 1)
    sub = lax.broadcasted_iota(jnp.int32, (LANES, L), 0)
    gz = gz_ref[...]
    nw = nw_ref[...]
    hd = GDN_HEAD

    heads = range(hb)
    sls = [slice(j * hd, (j + 1) * hd) for j in heads]
    ks, kq, decay, rhs, q_dec, k_end, dec_last = [], [], [], [], [], [], []
    for j in heads:
        hg = hg0 + j
        sl = sls[j]
        beta = jnp.sum(jnp.where(lane == lane_b + hg, beta_all, 0.0), axis=1, keepdims=True)
        gc = jnp.sum(jnp.where(lane == lane_a + hg, gc_all, 0.0), axis=1, keepdims=True)
        gr = jnp.sum(jnp.where(sub == lane_a + hg, gc_all_t, 0.0), axis=0, keepdims=True)
        gc_last = gc[L - 1:L]
        q = qc[:, sl]
        k = kc[:, sl]
        q = q * lax.rsqrt(jnp.sum(q * q, axis=-1, keepdims=True) + L2_EPS) * (hd ** -0.5)
        k = k * lax.rsqrt(jnp.sum(k * k, axis=-1, keepdims=True) + L2_EPS)
        e_gc = jnp.exp(gc)
        k_beta = k * beta
        ks.append(k)
        kq.append(jnp.concatenate([k_beta, q], axis=0))
        decay.append(jnp.where(causal, jnp.exp(jnp.where(causal, gc - gr, 0.0)), 0.0))
        rhs.append(jnp.concatenate([vc[:, sl] * beta, k_beta * e_gc], axis=1))
        q_dec.append(q * e_gc)
        k_end.append(k * jnp.exp(gc_last - gc))
        dec_last.append(jnp.exp(gc_last))
    sc = [_bdot_nt(kq[j], ks[j]) for j in heads]
    xs = [-jnp.where(strict, sc[j][:L] * decay[j], 0.0) for j in heads]
    sol = _unit_lower_solve(xs, rhs, n_stage)
    s_old = [s_ref[0, 0, j] for j in heads]
    ws = [_bdot(jnp.concatenate([sol[j][:, hd:], q_dec[j]], axis=0), s_old[j]) for j in heads]
    v_new = [sol[j][:, :hd] - ws[j][:L] for j in heads]
    os_ = [ws[j][L:] + _bdot(sc[j][L:] * decay[j], v_new[j]) for j in heads]
    for j in heads:
        s_ref[0, 0, j] = dec_last[j] * s_old[j] + _bdot_tn(k_end[j], v_new[j])
    for j in heads:
        o = os_[j]
        rms = o * lax.rsqrt(jnp.mean(o * o, axis=-1, keepdims=True) + RMS_EPS) * nw
        z = gz[:, sls[j]]
        o_ref[:, sls[j]] = rms * (z * _sigmoid(z))


def _gdn_call(u_main, u_misc, conv_init, conv_w, prm, norm_w, s0, lyr, acc, lyr_out, *, grp, D, lane_b, lane_a, hb):
    row0, B, Tp, L, t_real = grp
    H = D // GDN_HEAD
    hb = min(hb, H)
    W = hb * GDN_HEAD
    nc = Tp // L
    rb = lambda b, c: (row0 + b * Tp) // L + c
    pb = lambda b, c: jnp.maximum((row0 + b * Tp + c * L) // SUBLANES - 1, 0)
    off = 3 * D // W
    seg = D // W
    cur = lambda s: pl.BlockSpec((L, W), lambda b, h, c: (rb(b, c), off + s * seg + h))
    prev = lambda s: pl.BlockSpec((SUBLANES, W), lambda b, h, c: (pb(b, c), off + s * seg + h))
    init = lambda s: pl.BlockSpec((1, SUBLANES, W), lambda b, h, c: (b, 0, s * seg + h))
    cw = lambda s: pl.BlockSpec((GDN_CONV, W), lambda b, h, c: (0, s * seg + h))
    n_stage = int(math.log2(L))
    assert 1 << n_stage == L
    return pl.pallas_call(
        functools.partial(_gdn_kernel, L=L, t_real=t_real, hb=hb, lane_b=lane_b, lane_a=lane_a, n_stage=n_stage),
        grid=(B, H // hb, nc),
        in_specs=[cur(0), cur(1), cur(2), prev(0), prev(1), prev(2), init(0), init(1), init(2),
                  cw(0), cw(1), cw(2),
                  pl.BlockSpec((L, W), lambda b, h, c: (rb(b, c), 6 * D // W + h)),
                  pl.BlockSpec((L, LANES), lambda b, h, c: (rb(b, c), 0)),
                  pl.BlockSpec((SUBLANES, LANES), lambda b, h, c: (0, 0)),
                  pl.BlockSpec((1, GDN_HEAD), lambda b, h, c: (0, 0)),
                  pl.BlockSpec((1, 1, hb, GDN_HEAD, GDN_HEAD), lambda b, h, c: (lyr, b, h, 0, 0)),
                  pl.BlockSpec(memory_space=pl.ANY)],
        out_specs=[pl.BlockSpec((L, W), lambda b, h, c: (b * nc + c, h)),
                   pl.BlockSpec((1, 1, hb, GDN_HEAD, GDN_HEAD), lambda b, h, c: (lyr_out, b, h, 0, 0))],
        out_shape=[jax.ShapeDtypeStruct((B * Tp, D), F32),
                   jax.ShapeDtypeStruct(acc.shape, F32)],
        input_output_aliases={17: 1},
        compiler_params=_params(("parallel", "parallel", "arbitrary")),
        name="gdn_mixer",
    )(u_main, u_main, u_main, u_main, u_main, u_main, conv_init, conv_init, conv_init,
      conv_w, conv_w, conv_w, u_main, u_misc, prm, norm_w, s0, acc)


def _rwkv_kernel(r_ref, k_ref, v_ref, pr_ref, pk_ref, pv_ref, ir_ref, ik_ref, iv_ref,
                 misc_ref, pmisc_ref, imisc_ref, mur_ref, muk_ref, muv_ref, mum_ref,
                 w0_ref, w2_ref, a0_ref, a2_ref, g2_ref, kk_ref, ka_ref, rk_ref, lnw_ref, lnb_ref, s0_ref,
                 acc_ref, o_ref, s_ref, *, L, t_real, hb, n_stage):
    del acc_ref
    c = pl.program_id(2)

    @pl.when(c == 0)
    def _():
        s_ref[0, 0] = s0_ref[0, 0]

    row = lax.broadcasted_iota(jnp.int32, (L, 1), 0)
    valid = (c * L + row) < t_real
    causal, _, _ = _tri_masks(L)

    def mix(cur_ref, prev_ref, init_ref, mu_ref):
        cur = cur_ref[...]
        prev8 = jnp.where(c == 0, init_ref[0], prev_ref[...])
        ext = jnp.concatenate([prev8, cur], axis=0)
        prev = pltpu.roll(ext, 1, 0)[SUBLANES:]
        return cur + (prev - cur) * mu_ref[...]

    r = mix(r_ref, pr_ref, ir_ref, mur_ref)
    k = mix(k_ref, pk_ref, ik_ref, muk_ref)
    v = mix(v_ref, pv_ref, iv_ref, muv_ref)
    m = mix(misc_ref, pmisc_ref, imisc_ref, mum_ref)
    wl = m[:, MISC_WL:MISC_AL]
    al = m[:, MISC_AL:MISC_GL]
    gl = m[:, MISC_GL:]

    w_log = -_softplus(-(w0_ref[...] + _bdot(jnp.tanh(wl), w2_ref[...]))) - 0.5
    lw = jnp.where(valid, -jnp.exp(w_log), 0.0)
    a_sig = _sigmoid(a0_ref[...] + _bdot(al, a2_ref[...]))
    gate = _bdot(_sigmoid(gl), g2_ref[...])
    kk = k * kk_ref[...]
    k2 = jnp.where(valid, k * (1.0 + (a_sig - 1.0) * ka_ref[...]), 0.0)
    rk = rk_ref[...]
    lnw = lnw_ref[...]
    lnb = lnb_ref[...]
    cw = _cumsum_rows(causal.astype(BF16), lw)
    hd = RWKV_HEAD
    ri2 = lax.broadcasted_iota(jnp.int32, (2 * L, 2 * L), 0)
    ci2 = lax.broadcasted_iota(jnp.int32, (2 * L, 2 * L), 1)
    mask2 = ((ri2 & (L - 1)) - (ci2 & (L - 1)) + jnp.where(ri2 >= L, 1, 0)) > 0

    heads = range(hb)
    sls = [slice(j * hd, (j + 1) * hd) for j in heads]
    ar, bk, end, w_last = [], [], [], []
    for sl in sls:
        cw_h = cw[:, sl]
        cw_last = cw_h[L - 1:L]
        kk_h = kk[:, sl]
        kk_h = jnp.where(valid, kk_h * lax.rsqrt(jnp.sum(kk_h * kk_h, axis=-1, keepdims=True) + L2_EPS), 0.0)
        b_h = kk_h * a_sig[:, sl]
        e_neg = jnp.exp(-cw_h)
        e_end = jnp.exp(cw_last - cw_h)
        ar.append(jnp.concatenate([-kk_h * jnp.exp(cw_h - lw[:, sl]), r[:, sl] * jnp.exp(cw_h)], axis=0))
        bk.append(jnp.concatenate([b_h * e_neg, k2[:, sl] * e_neg], axis=0))
        end.append(jnp.concatenate([b_h * e_end, k2[:, sl] * e_end], axis=0))
        w_last.append(jnp.exp(cw_last))
    s_old = [s_ref[0, 0, j] for j in heads]
    g = [jnp.where(mask2, _bdot_nt(ar[j], bk[j]), 0.0) for j in heads]
    ars = [_bdot_nt(ar[j], s_old[j]) for j in heads]
    akv = [_bdot(g[j][:, L:], v[:, sls[j]]) for j in heads]
    u = _unit_lower_solve([g[j][:L, :L] for j in heads], [ars[j][:L] + akv[j][:L] for j in heads], n_stage)
    ys = [ars[j][L:] + akv[j][L:] + _bdot(g[j][L:, :L], u[j]) for j in heads]
    for j in heads:
        s_ref[0, 0, j] = s_old[j] * w_last[j] + _bdot_tn(jnp.concatenate([u[j], v[:, sls[j]]], axis=0), end[j])
    for j in heads:
        sl = sls[j]
        y = ys[j]
        mu = jnp.mean(y, axis=-1, keepdims=True)
        var = jnp.mean(jnp.square(y - mu), axis=-1, keepdims=True)
        yn = (y - mu) * lax.rsqrt(var + RWKV_GN_EPS) * lnw[:, sl] + lnb[:, sl]
        bonus = jnp.sum(r[:, sl] * k2[:, sl] * rk[:, sl], axis=-1, keepdims=True) * v[:, sl]
        o_ref[:, sl] = (yn + bonus) * gate[:, sl]


def _rwkv_call(u_main, u_misc, init_main, init_misc, pr, s0, lyr, acc, lyr_out, *, grp, D, hb):
    row0, B, Tp, L, t_real = grp
    H = D // RWKV_HEAD
    hb = min(hb, H)
    W = hb * RWKV_HEAD
    nc = Tp // L
    cm = u_misc.shape[1]
    rb = lambda b, c: (row0 + b * Tp) // L + c
    pb = lambda b, c: jnp.maximum((row0 + b * Tp + c * L) // SUBLANES - 1, 0)
    off = 7 * D // W
    seg = D // W
    cur = lambda s: pl.BlockSpec((L, W), lambda b, h, c: (rb(b, c), off + s * seg + h))
    prev = lambda s: pl.BlockSpec((SUBLANES, W), lambda b, h, c: (pb(b, c), off + s * seg + h))
    init = lambda s: pl.BlockSpec((1, SUBLANES, W), lambda b, h, c: (b, 0, s * seg + h))
    vec = lambda s: pl.BlockSpec((1, W), lambda b, h, c: (0, s * seg + h))
    full = lambda rows: pl.BlockSpec((rows, W), lambda b, h, c: (0, h))
    n_stage = int(math.log2(L))
    assert 1 << n_stage == L
    return pl.pallas_call(
        functools.partial(_rwkv_kernel, L=L, t_real=t_real, hb=hb, n_stage=n_stage),
        grid=(B, H // hb, nc),
        in_specs=[cur(0), cur(1), cur(2), prev(0), prev(1), prev(2), init(0), init(1), init(2),
                  pl.BlockSpec((L, cm), lambda b, h, c: (rb(b, c), 0)),
                  pl.BlockSpec((SUBLANES, cm), lambda b, h, c: (pb(b, c), 0)),
                  pl.BlockSpec((1, SUBLANES, cm), lambda b, h, c: (b, 0, 0)),
                  vec(0), vec(1), vec(2),
                  pl.BlockSpec((1, cm), lambda b, h, c: (0, 0)),
                  vec(0), full(LANES), vec(0), full(LANES), full(cm - MISC_GL),
                  vec(0), vec(0), vec(0), vec(0), vec(0),
                  pl.BlockSpec((1, 1, hb, RWKV_HEAD, RWKV_HEAD), lambda b, h, c: (lyr, b, h, 0, 0)),
                  pl.BlockSpec(memory_space=pl.ANY)],
        out_specs=[pl.BlockSpec((L, W), lambda b, h, c: (b * nc + c, h)),
                   pl.BlockSpec((1, 1, hb, RWKV_HEAD, RWKV_HEAD), lambda b, h, c: (lyr_out, b, h, 0, 0))],
        out_shape=[jax.ShapeDtypeStruct((B * Tp, D), F32),
                   jax.ShapeDtypeStruct(acc.shape, F32)],
        input_output_aliases={27: 1},
        compiler_params=_params(("parallel", "parallel", "arbitrary")),
        name="rwkv_mixer",
    )(u_main, u_main, u_main, u_main, u_main, u_main, init_main, init_main, init_main,
      u_misc, u_misc, init_misc, pr["mu_main"], pr["mu_main"], pr["mu_main"], pr["mu_misc"],
      pr["w0"], pr["w2"], pr["a0"], pr["a2"], pr["g2"], pr["k_k"], pr["k_a"], pr["r_k"],
      pr["ln_w"], pr["ln_b"], s0, acc)


def _merge_kernel(o0_ref, o1_ref, o2_ref, g0_ref, g1_ref, g2_ref, wb_ref, out_ref):
    acc = _sigmoid(g0_ref[...]) * _bdot(o0_ref[...], wb_ref[0])
    acc = acc + _sigmoid(g1_ref[...]) * _bdot(o1_ref[...], wb_ref[1])
    acc = acc + _sigmoid(g2_ref[...]) * _bdot(o2_ref[...], wb_ref[2])
    out_ref[...] = acc.astype(out_ref.dtype)


def _merge_call(o_gla, o_gdn, o_rwkv, u_main, w_branch, *, D, row0, tm, tn):
    n = o_gla.shape[0]
    gate_off = 10 * D // tn
    rb0 = row0 // tm
    o_spec = pl.BlockSpec((tm, D), lambda i, j: (i, 0))
    g_spec = lambda s: pl.BlockSpec((tm, tn), lambda i, j: (rb0 + i, gate_off + s * (D // tn) + j))
    return pl.pallas_call(
        _merge_kernel,
        grid=(n // tm, D // tn),
        in_specs=[o_spec, o_spec, o_spec, g_spec(0), g_spec(1), g_spec(2),
                  pl.BlockSpec((3, D, tn), lambda i, j: (0, 0, j))],
        out_specs=pl.BlockSpec((tm, tn), lambda i, j: (i, j)),
        out_shape=jax.ShapeDtypeStruct((n, D), BF16),
        compiler_params=_params(("parallel", "arbitrary")),
        name="branch_merge",
    )(o_gla, o_gdn, o_rwkv, u_main, u_main, u_main, w_branch)


def _layernorm(x, g, b):
    mu = jnp.mean(x, axis=-1, keepdims=True)
    var = jnp.mean(jnp.square(x - mu), axis=-1, keepdims=True)
    return (x - mu) * lax.rsqrt(var + LN_EPS) * g + b


def _outproj_kernel(m_ref, w_ref, x_ref, g_ref, b_ref, h_ref, *, alpha):
    y = alpha * x_ref[...] + jnp.dot(m_ref[...], w_ref[...], preferred_element_type=F32)
    h_ref[...] = _layernorm(y, g_ref[...], b_ref[...])


def _outproj_call(merged, w_out, x_pad, g, b, *, alpha, row0, tm):
    n, D = merged.shape
    rb0 = row0 // tm
    return pl.pallas_call(
        functools.partial(_outproj_kernel, alpha=alpha),
        grid=(n // tm,),
        in_specs=[pl.BlockSpec((tm, D), lambda i: (i, 0)),
                  pl.BlockSpec((D, D), lambda i: (0, 0)),
                  pl.BlockSpec((tm, D), lambda i: (rb0 + i, 0)),
                  pl.BlockSpec((1, D), lambda i: (0, 0)),
                  pl.BlockSpec((1, D), lambda i: (0, 0))],
        out_specs=pl.BlockSpec((tm, D), lambda i: (i, 0)),
        out_shape=jax.ShapeDtypeStruct((n, D), F32),
        compiler_params=_params(("parallel",)),
        name="out_proj_ln",
    )(merged, w_out, x_pad, g, b)


def _router_kernel(h_ref, wr_ref, bias_ref, ids_ref, wts_ref, *, n_exp):
    tm = h_ref.shape[0]
    logits = _bdot_nt(wr_ref[...], h_ref[...])[:n_exp]
    scores = _sigmoid(logits)
    sel = scores + bias_ref[...][:n_exp]
    gsz = n_exp // N_GROUPS
    neg = -jnp.inf

    def first_argmax(vals, iota, n):
        mx = jnp.max(vals, axis=0, keepdims=True)
        idx = jnp.min(jnp.where(vals == mx, iota, n), axis=0, keepdims=True)
        return mx, idx

    io_g = lax.broadcasted_iota(jnp.int32, (gsz, tm), 0)
    grp_rows = []
    for gi in range(N_GROUPS):
        blk = sel[gi * gsz:(gi + 1) * gsz]
        m1, i1 = first_argmax(blk, io_g, gsz)
        m2 = jnp.max(jnp.where(io_g == i1, neg, blk), axis=0, keepdims=True)
        grp_rows.append(m1 + m2)
    grp = jnp.concatenate(grp_rows, axis=0)
    io_n = lax.broadcasted_iota(jnp.int32, (N_GROUPS, tm), 0)
    gmask = jnp.zeros((N_GROUPS, tm), F32)
    for _ in range(TOPK_GROUPS):
        _, gi = first_argmax(grp, io_n, N_GROUPS)
        hit = io_n == gi
        gmask = jnp.where(hit, 1.0, gmask)
        grp = jnp.where(hit, neg, grp)
    io_e = lax.broadcasted_iota(jnp.int32, (n_exp, tm), 0)
    emask = jnp.concatenate(
        [jnp.broadcast_to(gmask[gi:gi + 1], (gsz, tm)) for gi in range(N_GROUPS)], axis=0)
    cand = jnp.where(emask > 0.5, sel, neg)
    ids, wts = [], []
    for _ in range(TOP_K):
        _, ei = first_argmax(cand, io_e, n_exp)
        hit = io_e == ei
        ids.append(ei)
        wts.append(jnp.sum(jnp.where(hit, scores, 0.0), axis=0, keepdims=True))
        cand = jnp.where(hit, neg, cand)
    w = jnp.concatenate(wts, axis=0)
    ids_ref[...] = jnp.concatenate(ids, axis=0)
    wts_ref[...] = w / (jnp.sum(w, axis=0, keepdims=True) + 1e-20) * ROUTED_SCALE


def _router_call(h, wr_t, bias_col, *, n_exp, tm):
    n, D = h.shape
    return pl.pallas_call(
        functools.partial(_router_kernel, n_exp=n_exp),
        grid=(n // tm,),
        in_specs=[pl.BlockSpec((tm, D), lambda i: (i, 0)),
                  pl.BlockSpec(wr_t.shape, lambda i: (0, 0)),
                  pl.BlockSpec(bias_col.shape, lambda i: (0, 0))],
        out_specs=[pl.BlockSpec((TOP_K, tm), lambda i: (0, i)),
                   pl.BlockSpec((TOP_K, tm), lambda i: (0, i))],
        out_shape=[jax.ShapeDtypeStruct((TOP_K, n), jnp.int32),
                   jax.ShapeDtypeStruct((TOP_K, n), F32)],
        compiler_params=_params(("parallel",)),
        name="router_topk",
    )(h, wr_t, bias_col)


DMA_UNROLL = 8
GDN_HEADS_PER_STEP = (16, 16)
RWKV_HEADS_PER_STEP = (32, 16)


def _swiglu(x_bf, w_gu, w_down):
    f = w_down.shape[0]
    gu = jnp.dot(x_bf, w_gu, preferred_element_type=F32)
    gt = gu[:, :f]
    act = gt * _sigmoid(gt) * gu[:, f:]
    return jnp.dot(act.astype(BF16), w_down, preferred_element_type=F32)


def _moe_kernel(blk_e_ref, n_used_ref, src_ref, nsrc_ref, h_hbm, sw_ref, wgu_ref, wd_ref, y_ref,
                xbuf, wgu_bf, wd_bf, sem_in, *, tm):
    i = pl.program_id(0)
    n_used = n_used_ref[0]
    slot = i % 2

    @pl.when(jnp.logical_or(i == 0, blk_e_ref[i] != blk_e_ref[jnp.maximum(i - 1, 0)]))
    def _():
        wgu_bf[...] = wgu_ref[0, 0].astype(BF16)
        wd_bf[...] = wd_ref[0, 0].astype(BF16)

    def gather(idx_ref, buf_slot, unroll):
        def body(r, carry):
            pltpu.make_async_copy(h_hbm.at[pl.ds(idx_ref[0, 0, r], 1)], xbuf.at[buf_slot, pl.ds(r, 1)],
                                  sem_in.at[buf_slot]).start()
            return carry
        lax.fori_loop(0, tm, body, 0, unroll=unroll)

    def wait(buf_slot):
        pltpu.make_async_copy(xbuf.at[buf_slot], xbuf.at[buf_slot], sem_in.at[buf_slot]).wait()

    @pl.when(i == 0)
    def _():
        gather(src_ref, 0, DMA_UNROLL)

    @pl.when(i < n_used)
    def _():
        wait(slot)
        gather(nsrc_ref, 1 - slot, True)
        y_ref[...] = _swiglu(xbuf[slot].astype(BF16), wgu_bf[...], wd_bf[...]) * sw_ref[...]

    @pl.when(i == n_used)
    def _():
        wait(slot)

    @pl.when(i >= n_used)
    def _():
        y_ref[...] = jnp.zeros_like(y_ref)


def _moe_call(blk_e, n_used, src, h, slot_w, w_gu, w_down, lyr, *, tm):
    n_blocks = src.shape[0]
    n, D = h.shape
    f2 = w_gu.shape[3]
    f = w_down.shape[2]
    grid_spec = pltpu.PrefetchScalarGridSpec(
        num_scalar_prefetch=2,
        grid=(n_blocks,),
        in_specs=[pl.BlockSpec((1, 1, tm), lambda i, be, nu: (i, 0, 0), memory_space=pltpu.SMEM),
                  pl.BlockSpec((1, 1, tm), lambda i, be, nu: (jnp.minimum(i + 1, n_blocks - 1), 0, 0),
                               memory_space=pltpu.SMEM),
                  pl.BlockSpec(memory_space=pl.ANY),
                  pl.BlockSpec((tm, 1), lambda i, be, nu: (i, 0)),
                  pl.BlockSpec((1, 1, D, f2), lambda i, be, nu: (lyr, be[i], 0, 0)),
                  pl.BlockSpec((1, 1, f, D), lambda i, be, nu: (lyr, be[i], 0, 0))],
        out_specs=pl.BlockSpec((tm, D), lambda i, be, nu: (i, 0)),
        scratch_shapes=[pltpu.VMEM((2, tm, D), F32), pltpu.VMEM((D, f2), BF16), pltpu.VMEM((f, D), BF16),
                        pltpu.SemaphoreType.DMA((2,))],
    )
    return pl.pallas_call(
        functools.partial(_moe_kernel, tm=tm),
        grid_spec=grid_spec,
        out_shape=jax.ShapeDtypeStruct((n_blocks * tm, D), F32),
        compiler_params=_params(("arbitrary",)),
        name="moe_experts",
    )(blk_e, n_used, src, src, h, slot_w, w_gu, w_down)


def _combine_kernel(dest_ref, y_hbm, h_ref, wgu_ref, wd_ref, g_ref, b_ref, o_ref, gbuf, sem, *, alpha, nk, tm):
    for k in range(nk):
        def body(r, carry, k=k):
            pltpu.make_async_copy(y_hbm.at[pl.ds(dest_ref[0, 0, k * tm + r], 1)], gbuf.at[k, pl.ds(r, 1)],
                                  sem).start()
            return carry
        lax.fori_loop(0, tm, body, 0, unroll=DMA_UNROLL)

    h = h_ref[...]
    acc = alpha * h + _swiglu(h.astype(BF16), wgu_ref[...], wd_ref[...])
    pltpu.make_async_copy(gbuf, gbuf, sem).wait()
    for k in range(nk):
        acc = acc + gbuf[k]
    o_ref[...] = _layernorm(acc, g_ref[...], b_ref[...])


def _combine_call(dest, y_sorted, h, w_sh_gu, w_sh_down, g, b, *, alpha, tm, nk):
    n, D = h.shape
    f2 = w_sh_gu.shape[1]
    f = w_sh_down.shape[0]
    return pl.pallas_call(
        functools.partial(_combine_kernel, alpha=alpha, nk=nk, tm=tm),
        grid=(n // tm,),
        in_specs=[pl.BlockSpec((1, 1, nk * tm), lambda i: (i, 0, 0), memory_space=pltpu.SMEM),
                  pl.BlockSpec(memory_space=pl.ANY),
                  pl.BlockSpec((tm, D), lambda i: (i, 0)),
                  pl.BlockSpec((D, f2), lambda i: (0, 0)),
                  pl.BlockSpec((f, D), lambda i: (0, 0)),
                  pl.BlockSpec((1, D), lambda i: (0, 0)),
                  pl.BlockSpec((1, D), lambda i: (0, 0))],
        out_specs=pl.BlockSpec((tm, D), lambda i: (i, 0)),
        out_shape=jax.ShapeDtypeStruct((n, D), F32),
        scratch_shapes=[pltpu.VMEM((nk, tm, D), F32), pltpu.SemaphoreType.DMA(())],
        compiler_params=_params(("arbitrary",)),
        name="moe_combine_ln",
    )(dest, y_sorted, h, w_sh_gu, w_sh_down, g, b)


def _dispatch_tables(ids, wts, *, n_exp_tot, tm, n_blocks):
    n, nk = ids.shape
    onehot = (ids[:, :, None] == jnp.arange(n_exp_tot, dtype=jnp.int32)).any(axis=1)
    mask = onehot.astype(jnp.int32)
    cum = jnp.cumsum(mask, axis=0) - mask
    pos = jnp.take_along_axis(cum, ids, axis=1)
    counts = mask.sum(axis=0)
    padded = (counts + tm - 1) // tm * tm
    pad_end = jnp.cumsum(padded)
    pad_start = pad_end - padded
    dest = pad_start[ids] + pos
    n_used = (pad_end[-1] // tm).astype(jnp.int32).reshape(1)
    p = n_blocks * tm
    slot_a = jnp.full((p,), -1, jnp.int32).at[dest.reshape(-1)].set(jnp.arange(n * nk, dtype=jnp.int32))
    used = slot_a >= 0
    a = jnp.maximum(slot_a, 0)
    src = jnp.where(used, a // nk, 0)
    slot_w = jnp.where(used, wts.reshape(-1)[a], 0.0)
    blk_start = jnp.arange(n_blocks, dtype=jnp.int32) * tm
    blk_e = jnp.minimum(jnp.sum((pad_end[None, :] <= blk_start[:, None]).astype(jnp.int32), axis=1), n_exp_tot - 1)
    dest_t = dest.reshape(n // tm, tm, nk).transpose(0, 2, 1).reshape(n // tm, 1, nk * tm)
    return blk_e, n_used, src.reshape(n_blocks, 1, tm), slot_w.reshape(p, 1), dest_t


def _pad_rows(a, lanes):
    return jnp.pad(a, ((0, 0), (0, lanes - a.shape[1])))


def _layer(x, lyr, st, w, *, dims):
    D, Bp, T, Bs, Ts, Tsp = dims
    n_p = Bp * T
    n_s = Bs * Ts
    n = n_p + n_s
    depth = w["w_in"].shape[0]
    alpha = (2 * depth) ** 0.25
    Lp = math.gcd(T, CHUNK)
    grp_p = (0, Bp, T, Lp, T)
    grp_s = (n_p, Bs, Tsp, Tsp, Ts)
    H_gdn = D // GDN_HEAD
    lora = w["gla_w_lora"].shape[1]
    lane_b, lane_a = lora, lora + H_gdn
    wl_n, al_n, gl_n = w["rwkv_w2"].shape[1], w["rwkv_a2"].shape[1], w["rwkv_g2"].shape[1]
    assert lane_a + H_gdn <= LANES and wl_n <= LANES and al_n <= LANES and gl_n % LANES == 0
    assert T % Lp == 0 and n_p % Tsp == 0 and Ts >= GDN_CONV - 1 and Tsp % SUBLANES == 0

    wi = w["w_in"][lyr]
    o_glr = 3 * D
    o_gdn = o_glr + lora
    o_gb = o_gdn + 4 * D
    o_ga = o_gb + H_gdn
    o_rw = o_ga + H_gdn
    o_wl = o_rw + 3 * D
    o_al = o_wl + wl_n
    o_gl = o_al + al_n
    o_gate = o_gl + gl_n
    w_main = jnp.concatenate([wi[:, :3 * D], wi[:, o_gdn:o_gdn + 4 * D], wi[:, o_rw:o_rw + 3 * D],
                              wi[:, o_gate:o_gate + 3 * D]], axis=1).astype(BF16)
    w_misc = jnp.concatenate([
        _pad_rows(jnp.concatenate([wi[:, o_glr:o_glr + lora], wi[:, o_gb:o_gb + 2 * H_gdn]], axis=1), LANES),
        _pad_rows(wi[:, o_wl:o_wl + wl_n], LANES), _pad_rows(wi[:, o_al:o_al + al_n], LANES),
        wi[:, o_gl:o_gl + gl_n]], axis=1).astype(BF16)
    cm = w_misc.shape[1]

    xs_pad = jnp.pad(x[n_p:].reshape(Bs, Ts, D), ((0, 0), (0, Tsp - Ts), (0, 0))).reshape(Bs * Tsp, D)
    x_pad = jnp.concatenate([x[:n_p], xs_pad], axis=0)
    np_rows = x_pad.shape[0]
    x_bf = x_pad.astype(BF16)

    tm = _pick(np_rows, (1024, 512, 256, 128, 96, 64, 32, 16, 8))
    u_main = _matmul(x_bf, w_main, tm, _pick(w_main.shape[1], (1024, 512, 256, 128)))
    u_misc = _matmul(x_bf, w_misc, tm, cm)

    wl_ext = jnp.pad(w["gla_w_lora"][lyr], ((0, LANES - lora), (0, 0))).astype(BF16)
    b_lora = w["gla_b_lora"][lyr][None]
    gla_nw = w["gla_norm_w"][lyr][None]
    conv_w = w["gdn_conv_w"][lyr]
    gdn_prm = jnp.zeros((SUBLANES, LANES), F32)
    gdn_prm = gdn_prm.at[0, lane_a:lane_a + H_gdn].set(w["gdn_A_log"][lyr])
    gdn_prm = gdn_prm.at[1, lane_a:lane_a + H_gdn].set(w["gdn_dt_bias"][lyr])
    gdn_nw = w["gdn_norm_w"][lyr][None]
    mu = w["rwkv_mu"][lyr]

    def misc_layout(a):
        z = jnp.zeros(a.shape[:-1] + (LANES,), a.dtype)
        return jnp.concatenate([z, _pad_last(a[..., :wl_n]), _pad_last(a[..., wl_n:wl_n + al_n]),
                                a[..., wl_n + al_n:]], axis=-1)

    def _pad_last(a):
        return jnp.pad(a, [(0, 0)] * (a.ndim - 1) + [(0, LANES - a.shape[-1])])

    rw = {
        "mu_main": mu[None, :3 * D], "mu_misc": misc_layout(mu[None, 3 * D:]),
        "w0": w["rwkv_w0"][lyr][None],
        "w2": jnp.pad(w["rwkv_w2"][lyr], ((0, LANES - wl_n), (0, 0))).astype(BF16),
        "a0": w["rwkv_a0"][lyr][None],
        "a2": jnp.pad(w["rwkv_a2"][lyr], ((0, LANES - al_n), (0, 0))).astype(BF16),
        "g2": w["rwkv_g2"][lyr].astype(BF16),
        "k_k": w["rwkv_k_k"][lyr][None], "k_a": w["rwkv_k_a"][lyr][None],
        "r_k": w["rwkv_r_k"][lyr].reshape(1, D),
        "ln_w": w["rwkv_ln_w"][lyr][None], "ln_b": w["rwkv_ln_b"][lyr][None],
    }

    w_branch = w["w_branch"][lyr].astype(BF16)
    w_out = w["w_out"][lyr].astype(BF16)
    ln1_g, ln1_b = w["ln1_g"][lyr][None], w["ln1_b"][lyr][None]
    h_groups = []
    new_states = []
    for grp, s in ((grp_p, st["p"]), (grp_s, st["s"])):
        row0, B, Tp, L, t_real = grp
        l_idx = lyr if s["layered"] else 0
        conv_init = jnp.pad(s["conv"], ((0, 0), (SUBLANES - (GDN_CONV - 1), 0), (0, 0)))
        shift = s["shift"]
        init_main = jnp.pad(shift[:, None, :3 * D], ((0, 0), (SUBLANES - 1, 0), (0, 0)))
        init_misc = jnp.pad(misc_layout(shift[:, None, 3 * D:]), ((0, 0), (SUBLANES - 1, 0), (0, 0)))
        acc = s["acc"]
        o_gla, s_gla = _gla_call(u_main, u_misc, wl_ext, b_lora, gla_nw, s["gla"], l_idx, acc[0], lyr, grp=grp, D=D,
                                 hb=GLA_HEADS)
        o_gdn, s_gdn = _gdn_call(u_main, u_misc, conv_init, conv_w, gdn_prm, gdn_nw, s["gdn"], l_idx, acc[1], lyr,
                                 grp=grp, D=D, lane_b=lane_b, lane_a=lane_a, hb=GDN_HEADS_PER_STEP[L >= CHUNK])
        o_rwkv, s_rwkv = _rwkv_call(u_main, u_misc, init_main, init_misc, rw, s["rwkv"], l_idx, acc[2], lyr,
                                    grp=grp, D=D, hb=RWKV_HEADS_PER_STEP[L >= CHUNK])
        tm2 = _pick(math.gcd(B * Tp, row0) if row0 else B * Tp, (512, 256, 128, 64, 32, 16))
        merged = _merge_call(o_gla, o_gdn, o_rwkv, u_main, w_branch, D=D, row0=row0, tm=tm2,
                             tn=_pick(D, (512, 256, 128)))
        h_groups.append(_outproj_call(merged, w_out, x_pad, ln1_g, ln1_b, alpha=alpha, row0=row0, tm=tm2))
        nc_rows = GDN_CONV - 1
        last_rows = (row0 + np.arange(B)[:, None] * Tp + np.arange(t_real - nc_rows, t_real)[None, :]).reshape(-1)
        u_last = jnp.take(u_main, jnp.asarray(last_rows, jnp.int32), axis=0).reshape(B, nc_rows, -1)
        m_last = jnp.take(u_misc, jnp.asarray(last_rows[nc_rows - 1::nc_rows], jnp.int32), axis=0)
        new_conv = u_last[:, :, 3 * D:6 * D]
        new_shift = jnp.concatenate([u_last[:, -1, 7 * D:10 * D], m_last[:, MISC_WL:MISC_WL + wl_n],
                                     m_last[:, MISC_AL:MISC_AL + al_n], m_last[:, MISC_GL:]], axis=-1)
        new_states.append((s_gla, s_gdn, new_conv, s_rwkv, new_shift))

    h = jnp.concatenate([h_groups[0], h_groups[1].reshape(Bs, Tsp, D)[:, :Ts].reshape(n_s, D)], axis=0)

    n_exp = w["w_router"].shape[2]
    wr_t = jnp.pad(w["w_router"][lyr].T, ((0, LANES - n_exp), (0, 0))).astype(BF16)
    bias_col = jnp.pad(w["router_bias"][lyr], (0, LANES - n_exp))[:, None]
    tmr = _pick(n, (512, 256, 128, 64, 32, 16, 8))
    ids_t, wts_t = _router_call(h, wr_t, bias_col, n_exp=n_exp, tm=tmr)
    ids, wts = ids_t.T, wts_t.T
    tme = _pick(n, (256, 128, 64, 32, 16, 8))
    n_blocks = -(-n * TOP_K // tme) + n_exp + 1
    blk_e, n_used, src, slot_w, dest_t = _dispatch_tables(ids, wts, n_exp_tot=n_exp, tm=tme, n_blocks=n_blocks)
    y_sorted = _moe_call(blk_e, n_used, src, h, slot_w, w["w_exp_gu"], w["w_exp_down"], lyr, tm=tme)
    x_out = _combine_call(dest_t, y_sorted, h, w["w_sh_gu"][lyr].astype(BF16), w["w_sh_down"][lyr].astype(BF16),
                          w["ln2_g"][lyr][None], w["ln2_b"][lyr][None], alpha=alpha, tm=tme, nk=TOP_K)
    return x_out, new_states


def kernel(x_prompt, x_sample, state_gla, state_gdn, state_gdn_conv, state_rwkv, state_rwkv_shift, w_in, gla_w_lora, gla_b_lora, gla_norm_w, gdn_conv_w, gdn_A_log, gdn_dt_bias, gdn_norm_w, rwkv_mu, rwkv_w0, rwkv_w2, rwkv_a0, rwkv_a2, rwkv_g2, rwkv_k_k, rwkv_k_a, rwkv_r_k, rwkv_ln_w, rwkv_ln_b, w_branch, w_out, ln1_g, ln1_b, w_router, router_bias, w_exp_gu, w_exp_down, w_sh_gu, w_sh_down, ln2_g, ln2_b):
    w = dict(w_in=w_in, gla_w_lora=gla_w_lora, gla_b_lora=gla_b_lora, gla_norm_w=gla_norm_w,
             gdn_conv_w=gdn_conv_w, gdn_A_log=gdn_A_log, gdn_dt_bias=gdn_dt_bias, gdn_norm_w=gdn_norm_w,
             rwkv_mu=rwkv_mu, rwkv_w0=rwkv_w0, rwkv_w2=rwkv_w2, rwkv_a0=rwkv_a0, rwkv_a2=rwkv_a2,
             rwkv_g2=rwkv_g2, rwkv_k_k=rwkv_k_k, rwkv_k_a=rwkv_k_a, rwkv_r_k=rwkv_r_k, rwkv_ln_w=rwkv_ln_w,
             rwkv_ln_b=rwkv_ln_b, w_branch=w_branch, w_out=w_out, ln1_g=ln1_g, ln1_b=ln1_b,
             w_router=w_router, router_bias=router_bias, w_exp_gu=w_exp_gu, w_exp_down=w_exp_down,
             w_sh_gu=w_sh_gu, w_sh_down=w_sh_down, ln2_g=ln2_g, ln2_b=ln2_b)
    Bp, T, D = x_prompt.shape
    Bs, Ts, _ = x_sample.shape
    depth = w_in.shape[0]
    Tsp = -(-Ts // SUBLANES) * SUBLANES
    dims = (D, Bp, T, Bs, Ts, Tsp)
    conv_cols = state_gdn_conv.shape[-1]
    shift_cols = state_rwkv_shift.shape[-1]
    zero_p = dict(layered=False,
                  gla=jnp.zeros((1, Bp) + state_gla.shape[2:], F32),
                  gdn=jnp.zeros((1, Bp) + state_gdn.shape[2:], F32),
                  rwkv=jnp.zeros((1, Bp) + state_rwkv.shape[2:], F32),
                  conv=jnp.zeros((Bp, GDN_CONV - 1, conv_cols), F32),
                  shift=jnp.zeros((Bp, shift_cols), F32))
    x = jnp.concatenate([x_prompt.reshape(Bp * T, D), x_sample.reshape(Bs * Ts, D)], axis=0)
    acc = {g: [jnp.zeros((depth, B) + s.shape[2:], F32) for s in (state_gla, state_gdn, state_rwkv)]
           for g, B in (("p", Bp), ("s", Bs))}
    small = {"p": [], "s": []}
    for lyr in range(depth):
        st = {"p": dict(zero_p, acc=acc["p"]),
              "s": dict(layered=True, gla=state_gla, gdn=state_gdn, rwkv=state_rwkv,
                        conv=state_gdn_conv[lyr], shift=state_rwkv_shift[lyr], acc=acc["s"])}
        x, new = _layer(x, lyr, st, w, dims=dims)
        for g, (s_gla, s_gdn, new_conv, s_rwkv, new_shift) in zip(("p", "s"), new):
            acc[g] = [s_gla, s_gdn, s_rwkv]
            small[g].append((new_conv, new_shift))

    def group_out(g):
        conv = jnp.stack([c for c, _ in small[g]])
        shift = jnp.stack([s for _, s in small[g]])
        return (acc[g][0], acc[g][1], conv, acc[g][2], shift)

    n_p = Bp * T
    return (x[:n_p].reshape(Bp, T, D), x[n_p:].reshape(Bs, Ts, D)) + group_out("p") + group_out("s")
```
